```python
import math
import jax, jax.numpy as jnp
from jax import lax
import numpy as np

D_MODEL = 2048
BATCH = 4
SEQ = 4096
DEPTH = 2

CTX_LEN = 256
GRID_W = 64
EPS = 1e-6
N_EVEN = (DEPTH + 1) // 2
N_ODD = DEPTH // 2

HEAD_DIM = 128
N_MIX_HEADS = D_MODEL // HEAD_DIM
DA_HEADS = N_MIX_HEADS // 2
DA_QK_DIM = HEAD_DIM // 2
DA_V_DIM = HEAD_DIM
Q_BLOCK = 128
ROPE_BASE = 10000.0
ROPE_AXIS_DIM = DA_QK_DIM // 2
ROPE_FREQS = ROPE_AXIS_DIM // 2
GM_GROUPS = N_MIX_HEADS - DA_HEADS
GM_GROUP_DIM = HEAD_DIM
GM_CHUNK = 128
DA_Q_W = DA_HEADS * 2 * DA_QK_DIM
DA_V_W = DA_HEADS * DA_V_DIM
GM_W = GM_GROUPS * GM_GROUP_DIM
EVEN_SPLITS = (DA_Q_W, 2 * DA_Q_W, 2 * DA_Q_W + DA_V_W, 2 * DA_Q_W + DA_V_W + GM_W)
EVEN_IN = 2 * DA_Q_W + DA_V_W + 2 * GM_W
MIX_OUT = DA_V_W + GM_W
SSD_D_INNER = 2 * D_MODEL
SSD_HEAD_DIM = 64
SSD_HEADS = SSD_D_INNER // SSD_HEAD_DIM
SSD_GROUPS = 8
SSD_HEADS_PER_GROUP = SSD_HEADS // SSD_GROUPS
SSD_STATE = 128
SSD_CONV_W = 7
SSD_CHUNK = 128
SSD_BC_W = 2 * 2 * SSD_GROUPS * SSD_STATE
SSD_CONV_CH = SSD_D_INNER + SSD_BC_W
SSD_DT_W = 2 * SSD_HEADS
ODD_IN = SSD_D_INNER + SSD_CONV_CH + SSD_DT_W
N_EXPERTS = 32
TOP_K = 4
D_EXPERT = D_MODEL
SWIGLU_LIMIT = 7.0
SWIGLU_ALPHA = 1.702
MOE_BLOCK = 128

kernel_name = "hybrid_diffattn_gmlp_ssd_moe_prefix_dit"

F32 = jnp.float32


def rms_norm(x, g):
    xf = x.astype(F32)
    y = xf * lax.rsqrt(jnp.mean(xf * xf, axis=-1, keepdims=True) + EPS)
    return (y * g.astype(F32)).astype(x.dtype)


def layer_norm(x, g, b):
    xf = x.astype(F32)
    mu = jnp.mean(xf, axis=-1, keepdims=True)
    var = jnp.mean(jnp.square(xf - mu), axis=-1, keepdims=True)
    return ((xf - mu) * lax.rsqrt(var + EPS) * g.astype(F32) + b.astype(F32)).astype(x.dtype)


def adaln(cond, w, b):
    m = jax.nn.silu(cond) @ w + b
    return jnp.split(m, 6, axis=-1)


def modulate(x, g, shift, scale):
    return rms_norm(x, g) * (1 + scale) + shift


def axial_rope_tables(n_tokens, dtype):
    rows = n_tokens // GRID_W
    row = jnp.repeat(jnp.arange(rows), GRID_W)
    col = jnp.tile(jnp.arange(GRID_W), rows)
    inv = ROPE_BASE ** (-jnp.arange(ROPE_FREQS, dtype=F32) * 2.0 / ROPE_AXIS_DIM)
    ang = jnp.stack([row, col], axis=-1).astype(F32)[..., None] * inv
    return jnp.cos(ang).astype(dtype), jnp.sin(ang).astype(dtype)


def apply_axial_rope(t, cos, sin):
    ts = t.reshape(t.shape[:-1] + (2, 2, ROPE_FREQS))
    t1, t2 = ts[..., 0, :], ts[..., 1, :]
    out = jnp.stack([t1 * cos - t2 * sin, t2 * cos + t1 * sin], axis=-2)
    return out.reshape(t.shape)


def diff_softmax_mix(qa, qb, ka, kb, v, lam):
    scale = DA_QK_DIM ** -0.5
    sa = jnp.einsum('bhqd,bhkd->bhqk', qa, ka).astype(F32) * scale
    sb = jnp.einsum('bhqd,bhkd->bhqk', qb, kb).astype(F32) * scale
    p = jax.nn.softmax(sa, axis=-1) - lam * jax.nn.softmax(sb, axis=-1)
    return jnp.einsum('bhqk,bhkd->bhqd', p.astype(v.dtype), v)


def diff_attention_blocks(qa, qb, ka, kb, v, lam):
    bsz, nh, n, _ = qa.shape
    nb = n // Q_BLOCK

    def blocks(t):
        return jnp.moveaxis(t.reshape(bsz, nh, nb, Q_BLOCK, t.shape[-1]), 2, 0)

    out = lax.map(lambda qq: diff_softmax_mix(qq[0], qq[1], ka, kb, v, lam), (blocks(qa), blocks(qb)))
    return jnp.moveaxis(out, 0, 2).reshape(bsz, nh, n, v.shape[-1])


def chunk_gmlp(u, v, ln_g, ln_b, w_s, b_s):
    bsz, n, _ = u.shape
    v = layer_norm(v.reshape(bsz, n, GM_GROUPS, GM_GROUP_DIM), ln_g, ln_b)
    v = v.reshape(bsz, n // GM_CHUNK, GM_CHUNK, GM_GROUPS, GM_GROUP_DIM)
    s = jnp.einsum('gpq,bcqgd->bcpgd', w_s, v) + jnp.swapaxes(b_s, 0, 1)[:, :, None]
    return u * s.reshape(bsz, n, GM_W)


def even_mixer(h, hc, w_in, w_o, q_norm_g, k_norm_g, lam_vecs, subln_g, gm_ln_g, gm_ln_b, gm_w_s, gm_b_s,
               layer_idx, with_ctx):
    bsz, n, _ = h.shape
    n_ctx = hc.shape[1]
    cos, sin = axial_rope_tables(n, h.dtype)
    lam_init = 0.8 - 0.6 * math.exp(-0.3 * layer_idx)
    lv = lam_vecs.astype(F32)
    lam = jnp.exp(jnp.sum(lv[0] * lv[1])) - jnp.exp(jnp.sum(lv[2] * lv[3])) + lam_init

    def qk_heads(t, g, m):
        t = rms_norm(t.reshape(bsz, m, DA_HEADS, 2, DA_QK_DIM), g)
        return jnp.transpose(t, (3, 0, 2, 1, 4))

    def v_heads(t, m):
        return jnp.transpose(t.reshape(bsz, m, DA_HEADS, DA_V_DIM), (0, 2, 1, 3))

    def diff_out(o, m):
        o = rms_norm(o, subln_g) * (1.0 - lam_init)
        return jnp.transpose(o, (0, 2, 1, 3)).reshape(bsz, m, DA_V_W)

    q, k, v, u, gv = jnp.split(h @ w_in, EVEN_SPLITS, axis=-1)
    qc, kc, vc, uc, gvc = jnp.split(hc @ w_in, EVEN_SPLITS, axis=-1)
    qa, qb = qk_heads(q, q_norm_g, n)
    ka, kb = qk_heads(k, k_norm_g, n)
    kca, kcb = qk_heads(kc, k_norm_g, n_ctx)
    vh, vch = v_heads(v, n), v_heads(vc, n_ctx)
    ka_all = jnp.concatenate([apply_axial_rope(ka, cos, sin), kca], axis=2)
    kb_all = jnp.concatenate([apply_axial_rope(kb, cos, sin), kcb], axis=2)
    v_all = jnp.concatenate([vh, vch], axis=2)
    o = diff_attention_blocks(apply_axial_rope(qa, cos, sin), apply_axial_rope(qb, cos, sin),
                              ka_all, kb_all, v_all, lam)
    gm = chunk_gmlp(jax.nn.gelu(u, approximate=False), jax.nn.gelu(gv, approximate=False),
                    gm_ln_g, gm_ln_b, gm_w_s, gm_b_s)
    y = jnp.concatenate([diff_out(o, n), gm], axis=-1) @ w_o
    if not with_ctx:
        return y, None
    qca, qcb = qk_heads(qc, q_norm_g, n_ctx)
    oc = diff_softmax_mix(qca, qcb, kca, kcb, vch, lam)
    gmc = chunk_gmlp(jax.nn.gelu(uc, approximate=False), jax.nn.gelu(gvc, approximate=False),
                     gm_ln_g, gm_ln_b, gm_w_s, gm_b_s)
    yc = jnp.concatenate([diff_out(oc, n_ctx), gmc], axis=-1) @ w_o
    return y, yc


def dwconv_centred(x, w, b):
    ch = x.shape[-1]
    pad = SSD_CONV_W // 2
    y = lax.conv_general_dilated(x, w[:, None, :].astype(x.dtype), window_strides=(1,), padding=((pad, pad),),
                                 dimension_numbers=('NWC', 'WIO', 'NWC'), feature_group_count=ch)
    return y + b


def ssd_project(h, w_in, conv_w, conv_b):
    bsz, n, _ = h.shape
    zz = h @ w_in
    z = zz[..., :SSD_D_INNER]
    xbc = jax.nn.silu(dwconv_centred(zz[..., SSD_D_INNER:SSD_D_INNER + SSD_CONV_CH], conv_w, conv_b))
    dt_raw = zz[..., SSD_D_INNER + SSD_CONV_CH:].reshape(bsz, n, 2, SSD_GROUPS, SSD_HEADS_PER_GROUP)
    xs = xbc[..., :SSD_D_INNER].reshape(bsz, n, SSD_GROUPS, SSD_HEADS_PER_GROUP, SSD_HEAD_DIM)
    bc = xbc[..., SSD_D_INNER:].reshape(bsz, n, 2, 2, SSD_GROUPS, SSD_STATE)
    return z, xs, bc, dt_raw


def ssd_chunked_scan(x, dt, bm, cm, a_neg, h0, with_y):
    bsz, n = x.shape[:2]
    nc = n // SSD_CHUNK

    def chunks(t):
        return jnp.moveaxis(t.reshape((bsz, nc, SSD_CHUNK) + t.shape[2:]), 1, 0)

    xdt = x.astype(F32) * dt[..., None]
    a = dt * a_neg
    causal = jnp.tril(jnp.ones((SSD_CHUNK, SSD_CHUNK), bool))[None, :, :, None, None]

    def step(hs, inp):
        xc, ac, bc, cc = inp
        acum = jnp.cumsum(ac, axis=1)
        a_last = acum[:, -1]
        decay_to_end = jnp.exp(a_last[:, None] - acum)
        h_new = hs * jnp.exp(a_last)[..., None, None] + jnp.einsum('bsgn,bsgj,bsgjp->bgjpn', bc, decay_to_end, xc)
        if not with_y:
            return h_new, None
        seg = acum[:, :, None] - acum[:, None]
        lmat = jnp.exp(jnp.where(causal, seg, -jnp.inf))
        cb = jnp.einsum('blgn,bsgn->blsg', cc, bc)
        y_diag = jnp.einsum('blsg,blsgj,bsgjp->blgjp', cb, lmat, xc)
        y_off = jnp.einsum('blgn,bgjpn->blgjp', cc, hs) * jnp.exp(acum)[..., None]
        return h_new, y_diag + y_off

    h_fin, ys = lax.scan(step, h0, (chunks(xdt), chunks(a), chunks(bm.astype(F32)), chunks(cm.astype(F32))))
    if with_y:
        ys = jnp.moveaxis(ys, 0, 1).reshape(x.shape)
    return ys, h_fin


def ssd_direction(xs, bc, dt_raw, dt_bias, a_neg, h0, d, with_y):
    dt = jax.nn.softplus(dt_raw[:, :, d].astype(F32) + dt_bias.astype(F32))
    args = (xs, dt, bc[:, :, d, 0], bc[:, :, d, 1])
    if d == 1:
        args = tuple(jnp.flip(t, axis=1) for t in args)
    y, h_fin = ssd_chunked_scan(args[0], args[1], args[2], args[3], a_neg, h0, with_y)
    if with_y and d == 1:
        y = jnp.flip(y, axis=1)
    return y, h_fin


def gated_group_rms_norm(y, z, g):
    t = (y * jax.nn.silu(z)).astype(F32)
    sh = t.shape
    t = t.reshape(sh[:-1] + (SSD_GROUPS, sh[-1] // SSD_GROUPS))
    t = t * lax.rsqrt(jnp.mean(t * t, axis=-1, keepdims=True) + EPS)
    return (t.reshape(sh) * g.astype(F32)).astype(z.dtype)


def ssd_mixer(h, hc, w_in, conv_w, conv_b, dt_bias, a_log, d_skip, norm_g, w_o, with_ctx):
    z, xs, bc, dtr = ssd_project(h, w_in, conv_w, conv_b)
    zc, xsc, bcc, dtrc = ssd_project(hc, w_in, conv_w, conv_b)
    h0 = jnp.zeros((h.shape[0], SSD_GROUPS, SSD_HEADS_PER_GROUP, SSD_HEAD_DIM, SSD_STATE), F32)
    skip = d_skip.astype(F32)[:, :, None]
    y = skip * xs.astype(F32)
    yc = skip * xsc.astype(F32) if with_ctx else None
    for d in range(2):
        a_neg = -jnp.exp(a_log[d].astype(F32))
        yc_d, h_ctx = ssd_direction(xsc, bcc, dtrc, dt_bias[d], a_neg, h0, d, with_ctx)
        y_d, _ = ssd_direction(xs, bc, dtr, dt_bias[d], a_neg, h_ctx, d, True)
        y = y + y_d
        if with_ctx:
            yc = yc + yc_d
    bsz, n = h.shape[:2]
    out = gated_group_rms_norm(y.reshape(bsz, n, SSD_D_INNER).astype(z.dtype), z, norm_g) @ w_o
    if not with_ctx:
        return out, None
    outc = gated_group_rms_norm(yc.reshape(bsz, hc.shape[1], SSD_D_INNER).astype(zc.dtype), zc, norm_g) @ w_o
    return out, outc


def moe_ffn(h, router_w, router_b, w1, b1, w2, b2):
    n_tok, d = h.shape
    logits = (h @ router_w).astype(F32) + router_b.astype(F32)
    top_val, top_idx = lax.top_k(logits, TOP_K)
    gate = jax.nn.softmax(top_val, axis=-1)
    n_asg = n_tok * TOP_K
    flat_e = top_idx.reshape(-1)
    flat_tok = jnp.repeat(jnp.arange(n_tok, dtype=jnp.int32), TOP_K)
    order = jnp.argsort(flat_e)
    e_sorted = flat_e[order]
    counts = jnp.bincount(flat_e, length=N_EXPERTS)
    padded = (counts + MOE_BLOCK - 1) // MOE_BLOCK * MOE_BLOCK
    pad_end = jnp.cumsum(padded)
    pad_start = pad_end - padded
    grp_start = jnp.cumsum(counts) - counts
    dest = pad_start[e_sorted] + jnp.arange(n_asg, dtype=jnp.int32) - grp_start[e_sorted]
    n_rows = -(-(n_asg + N_EXPERTS * (MOE_BLOCK - 1)) // MOE_BLOCK) * MOE_BLOCK
    n_blocks = n_rows // MOE_BLOCK
    row_tok = jnp.full((n_rows,), n_tok, jnp.int32).at[dest].set(flat_tok[order])
    row_gate = jnp.zeros((n_rows,), F32).at[dest].set(gate.reshape(-1)[order])
    blk_expert = jnp.minimum(jnp.searchsorted(pad_end, jnp.arange(n_blocks, dtype=jnp.int32) * MOE_BLOCK,
                                              side='right'), N_EXPERTS - 1)
    h_pad = jnp.concatenate([h, jnp.zeros((1, d), h.dtype)], axis=0)
    xs = h_pad[row_tok].reshape(n_blocks, MOE_BLOCK, d)

    def expert_block(args):
        xb, e = args
        gu = xb @ w1[e] + b1[e]
        g = jnp.minimum(gu[..., :D_EXPERT], SWIGLU_LIMIT)
        lin = jnp.clip(gu[..., D_EXPERT:], -SWIGLU_LIMIT, SWIGLU_LIMIT)
        act = g * jax.nn.sigmoid(SWIGLU_ALPHA * g) * (lin + 1)
        return act @ w2[e] + b2[e]

    ys = lax.map(expert_block, (xs, blk_expert)).reshape(n_rows, d)
    ys = ys * row_gate[:, None].astype(ys.dtype)
    return jnp.zeros((n_tok + 1, d), ys.dtype).at[row_tok].add(ys)[:n_tok]


def setup_inputs(seed: int = 0) -> dict:
    key = jax.random.key(seed)
    ks = iter(jax.random.split(key, 48))

    def nrm(shape, scale):
        return jax.random.normal(next(ks), shape, F32) * scale

    D = D_MODEL
    dt0 = jnp.exp(jax.random.uniform(next(ks), (N_ODD, 2, SSD_GROUPS, SSD_HEADS_PER_GROUP), F32,
                                     minval=math.log(1e-3), maxval=math.log(1e-1)))
    dt0 = jnp.maximum(dt0, 1e-4)
    return {
        "x": nrm((BATCH, SEQ, D), 1.0),
        "c": nrm((BATCH, D), 1.0),
        "ctx": nrm((BATCH, CTX_LEN, D), 1.0),
        "c_ctx": nrm((D,), 1.0),
        "ada_w": nrm((DEPTH, D, 6 * D), 0.5 * D ** -0.5),
        "ada_b": nrm((DEPTH, 6 * D), 0.02),
        "norm_g": 1.0 + nrm((DEPTH, 2, D), 0.02),
        "attn_w_in": nrm((N_EVEN, D, EVEN_IN), D ** -0.5),
        "attn_w_o": nrm((N_EVEN, MIX_OUT, D), MIX_OUT ** -0.5),
        "da_q_norm_g": 1.0 + nrm((N_EVEN, DA_QK_DIM), 0.02),
        "da_k_norm_g": 1.0 + nrm((N_EVEN, DA_QK_DIM), 0.02),
        "da_lambda": nrm((N_EVEN, 4, DA_QK_DIM), 0.1),
        "da_subln_g": 1.0 + nrm((N_EVEN, DA_V_DIM), 0.02),
        "gm_ln_g": 1.0 + nrm((N_EVEN, GM_GROUPS, GM_GROUP_DIM), 0.02),
        "gm_ln_b": nrm((N_EVEN, GM_GROUPS, GM_GROUP_DIM), 0.02),
        "gm_w_s": nrm((N_EVEN, GM_GROUPS, GM_CHUNK, GM_CHUNK), GM_CHUNK ** -0.5),
        "gm_b_s": 1.0 + nrm((N_EVEN, GM_GROUPS, GM_CHUNK), 0.02),
        "ssd_w_in": nrm((N_ODD, D, ODD_IN), D ** -0.5),
        "ssd_conv_w": nrm((N_ODD, SSD_CONV_W, SSD_CONV_CH), SSD_CONV_W ** -0.5),
        "ssd_conv_b": nrm((N_ODD, SSD_CONV_CH), 0.02),
        "ssd_dt_bias": dt0 + jnp.log(-jnp.expm1(-dt0)),
        "ssd_a_log": jnp.log(jax.random.uniform(next(ks), (N_ODD, 2, SSD_GROUPS, SSD_HEADS_PER_GROUP), F32,
                                                minval=1.0, maxval=16.0)),
        "ssd_d_skip": 1.0 + nrm((N_ODD, SSD_GROUPS, SSD_HEADS_PER_GROUP), 0.02),
        "ssd_norm_g": 1.0 + nrm((N_ODD, SSD_D_INNER), 0.02),
        "ssd_w_o": nrm((N_ODD, SSD_D_INNER, D), SSD_D_INNER ** -0.5),
        "router_w": nrm((DEPTH, D, N_EXPERTS), D ** -0.5),
        "router_b": nrm((DEPTH, N_EXPERTS), 0.01),
        "moe_w1": nrm((DEPTH, N_EXPERTS, D, 2 * D_EXPERT), D ** -0.5),
        "moe_b1": nrm((DEPTH, N_EXPERTS, 2 * D_EXPERT), 0.02),
        "moe_w2": nrm((DEPTH, N_EXPERTS, D_EXPERT, D), D_EXPERT ** -0.5),
        "moe_b2": nrm((DEPTH, N_EXPERTS, D), 0.02),
    }


def reference(x, c, ctx, c_ctx, ada_w, ada_b, norm_g, attn_w_in, attn_w_o, da_q_norm_g, da_k_norm_g, da_lambda,
              da_subln_g, gm_ln_g, gm_ln_b, gm_w_s, gm_b_s, ssd_w_in, ssd_conv_w, ssd_conv_b, ssd_dt_bias, ssd_a_log,
              ssd_d_skip, ssd_norm_g, ssd_w_o, router_w, router_b, moe_w1, moe_b1, moe_w2, moe_b2):
    bsz, n, d = x.shape
    n_ctx = ctx.shape[1]
    xc = ctx
    for li in range(DEPTH):
        last = li == DEPTH - 1
        lat_mod = [m[:, None, :] for m in adaln(c, ada_w[li], ada_b[li])]
        ctx_mod = adaln(c_ctx, ada_w[li], ada_b[li])
        h = modulate(x, norm_g[li, 0], lat_mod[0], lat_mod[1])
        hc = modulate(xc, norm_g[li, 0], ctx_mod[0], ctx_mod[1])
        j = li // 2
        if li % 2 == 0:
            y, yc = even_mixer(h, hc, attn_w_in[j], attn_w_o[j], da_q_norm_g[j], da_k_norm_g[j], da_lambda[j],
                               da_subln_g[j], gm_ln_g[j], gm_ln_b[j], gm_w_s[j], gm_b_s[j], li, not last)
        else:
            y, yc = ssd_mixer(h, hc, ssd_w_in[j], ssd_conv_w[j], ssd_conv_b[j], ssd_dt_bias[j], ssd_a_log[j],
                              ssd_d_skip[j], ssd_norm_g[j], ssd_w_o[j], not last)
        x = x + lat_mod[2] * y
        h = modulate(x, norm_g[li, 1], lat_mod[3], lat_mod[4])
        if last:
            f = moe_ffn(h.reshape(bsz * n, d), router_w[li], router_b[li], moe_w1[li], moe_b1[li], moe_w2[li],
                        moe_b2[li])
            x = x + lat_mod[5] * f.reshape(bsz, n, d)
        else:
            xc = xc + ctx_mod[2] * yc
            hc = modulate(xc, norm_g[li, 1], ctx_mod[3], ctx_mod[4])
            tokens = jnp.concatenate([h.reshape(bsz * n, d), hc.reshape(bsz * n_ctx, d)], axis=0)
            f = moe_ffn(tokens, router_w[li], router_b[li], moe_w1[li], moe_b1[li], moe_w2[li], moe_b2[li])
            x = x + lat_mod[5] * f[:bsz * n].reshape(bsz, n, d)
            xc = xc + ctx_mod[5] * f[bsz * n:].reshape(bsz, n_ctx, d)
    return x
```

```python
import functools
import math

import jax
import jax.numpy as jnp
from jax import lax
from jax.experimental import pallas as pl
from jax.experimental.pallas import tpu as pltpu

F32 = jnp.float32
BF16 = jnp.bfloat16
U32 = jnp.uint32
HIGHEST = lax.Precision.HIGHEST

D_MODEL = 2048
GRID_W = 64
EPS = 1e-6
LANES = 128
HEAD_DIM = 128
DA_HEADS = 8
DA_QK_DIM = 64
ROPE_BASE = 10000.0
ROPE_AXIS_DIM = DA_QK_DIM // 2
ROPE_FREQS = ROPE_AXIS_DIM // 2
GM_GROUPS = 8
GM_CHUNK = 128
DA_Q_W = DA_HEADS * 2 * DA_QK_DIM
DA_V_W = DA_HEADS * HEAD_DIM
GM_W = GM_GROUPS * HEAD_DIM
EVEN_IN = 2 * DA_Q_W + DA_V_W + 2 * GM_W
SSD_D_INNER = 2 * D_MODEL
SSD_HEAD_DIM = 64
SSD_HEADS = SSD_D_INNER // SSD_HEAD_DIM
SSD_GROUPS = 8
SSD_HPG = SSD_HEADS // SSD_GROUPS
SSD_GROUP_W = SSD_HPG * SSD_HEAD_DIM
SSD_STATE = 128
SSD_CONV_W = 7
SSD_CHUNK = 128
SSD_BC_W = 2 * 2 * SSD_GROUPS * SSD_STATE
SSD_CONV_CH = SSD_D_INNER + SSD_BC_W
SSD_DT_W = 2 * SSD_HEADS
N_EXPERTS = 32
TOP_K = 4
D_EXPERT = D_MODEL
SWIGLU_LIMIT = 7.0
SWIGLU_ALPHA = 1.702

MOE_TM = 1024
MOE_RB = 256
MOE_TF = 256
COMBINE_TC = 256
ROUTER_PAD = 128
MIB = 2 ** 20
V7X_VMEM_BYTES = 64 * MIB


def _params(sem, vmem_mib):
    assert vmem_mib * MIB < V7X_VMEM_BYTES
    return pltpu.CompilerParams(dimension_semantics=sem, vmem_limit_bytes=vmem_mib * MIB)


def _pow2_tile(limit, *sizes):
    t = limit
    while any(s % t for s in sizes):
        t //= 2
    return t


class _Geo:
    def __init__(self, b, s, c):
        self.b, self.s, self.c = b, s, c
        self.n_lat, self.n_ctx = b * s, b * c
        self.m = self.n_lat + self.n_ctx

    def mod_row(self, i, tm):
        return jnp.where(i < self.n_lat // tm, i // (self.s // tm), self.b)


def _adaln_kernel(c_ref, w_ref, b_ref, o_ref):
    c = c_ref[...]
    s = c * jax.nn.sigmoid(c)
    o_ref[0] = jnp.dot(s.astype(BF16), w_ref[0].astype(BF16), preferred_element_type=F32) + b_ref[0]


def _adaln_mods(cond, ada_w, ada_b):
    n_l, d, n = ada_w.shape
    r = cond.shape[0]
    tn = 1024
    return pl.pallas_call(
        _adaln_kernel,
        grid=(n_l, n // tn),
        in_specs=[
            pl.BlockSpec((r, d), lambda l, j: (0, 0)),
            pl.BlockSpec((1, d, tn), lambda l, j: (l, 0, j)),
            pl.BlockSpec((1, 1, tn), lambda l, j: (l, 0, j)),
        ],
        out_specs=pl.BlockSpec((1, r, tn), lambda l, j: (l, 0, j)),
        out_shape=jax.ShapeDtypeStruct((n_l, r, n), F32),
        compiler_params=_params(("arbitrary", "arbitrary"), 40),
        name="adaln",
    )(cond, ada_w, ada_b.reshape(n_l, 1, n))


def _modulated_norm(x, g, shift, scale):
    y = x * lax.rsqrt(jnp.mean(x * x, axis=-1, keepdims=True) + EPS)
    return y * g * (1.0 + scale) + shift


def _norm_proj_kernel(x_ref, g_ref, sh_ref, sc_ref, w_ref, o_ref, h_scr):
    @pl.when(pl.program_id(1) == 0)
    def _():
        h_scr[...] = _modulated_norm(x_ref[...], g_ref[...], sh_ref[0], sc_ref[0]).astype(BF16)

    o_ref[...] = jnp.dot(h_scr[...], w_ref[...], preferred_element_type=F32).astype(o_ref.dtype)


def _norm_proj(xa, geo, g, mod, w, out_dtype, tn):
    d = xa.shape[1]
    n = w.shape[1]
    tm = _pow2_tile(1024, geo.s, geo.n_ctx)
    row = lambda i, j: (geo.mod_row(i, tm), 0, 0)
    row_scale = lambda i, j: (geo.mod_row(i, tm), 0, 1)
    return pl.pallas_call(
        _norm_proj_kernel,
        grid=(geo.m // tm, n // tn),
        in_specs=[
            pl.BlockSpec((tm, d), lambda i, j: (i, 0)),
            pl.BlockSpec((1, d), lambda i, j: (0, 0)),
            pl.BlockSpec((1, 1, d), row),
            pl.BlockSpec((1, 1, d), row_scale),
            pl.BlockSpec((d, tn), lambda i, j: (0, j)),
        ],
        out_specs=pl.BlockSpec((tm, tn), lambda i, j: (i, j)),
        out_shape=jax.ShapeDtypeStruct((geo.m, n), out_dtype),
        scratch_shapes=[pltpu.VMEM((tm, d), BF16)],
        compiler_params=_params(("arbitrary", "arbitrary"), 48),
        name="norm_proj",
    )(xa, g.reshape(1, d), mod, mod, w)


def _qk_prep_kernel(t_ref, g_ref, cos_ref, sin_ref, o_ref):
    t = t_ref[...].astype(F32)
    lane = lax.broadcasted_iota(jnp.int32, t.shape, 1)
    first = lane < DA_QK_DIM
    sq = t * t
    sa = jnp.sum(jnp.where(first, sq, 0.0), axis=-1, keepdims=True)
    sb = jnp.sum(jnp.where(first, 0.0, sq), axis=-1, keepdims=True)
    ms = jnp.where(first, sa, sb) * (1.0 / DA_QK_DIM)
    y = t * lax.rsqrt(ms + EPS) * g_ref[0]
    partner = jnp.where((lane & ROPE_FREQS) == 0,
                        pltpu.roll(y, LANES - ROPE_FREQS, 1), pltpu.roll(y, ROPE_FREQS, 1))
    o_ref[...] = (y * cos_ref[...] + partner * sin_ref[...]).astype(BF16)


def _rope_tables(geo, tm):
    pos = jnp.arange(geo.s)
    inv = ROPE_BASE ** (-jnp.arange(ROPE_FREQS, dtype=F32) * 2.0 / ROPE_AXIS_DIM)
    ang = jnp.stack([pos // GRID_W, pos % GRID_W], axis=-1).astype(F32)[..., None] * inv
    cos, sin = jnp.cos(ang), jnp.sin(ang)
    cos64 = jnp.concatenate([cos, cos], axis=-1).reshape(geo.s, DA_QK_DIM)
    sin64 = jnp.concatenate([-sin, sin], axis=-1).reshape(geo.s, DA_QK_DIM)
    cos_t = jnp.concatenate([jnp.tile(cos64, (1, 2)), jnp.ones((tm, HEAD_DIM), F32)], axis=0)
    sin_t = jnp.concatenate([jnp.tile(sin64, (1, 2)), jnp.zeros((tm, HEAD_DIM), F32)], axis=0)
    return cos_t, sin_t


def _qk_prep(proj, geo, gq, gk):
    tm = _pow2_tile(512, geo.s, geo.n_ctx)
    cos_t, sin_t = _rope_tables(geo, tm)
    scale = DA_QK_DIM ** -0.5
    gains = jnp.stack([jnp.tile(gq.astype(F32) * scale, 2), jnp.tile(gk.astype(F32), 2)]).reshape(2, 1, HEAD_DIM)
    n_lat_tiles, per_seq = geo.n_lat // tm, geo.s // tm
    tab = lambda i, j: (jnp.where(i < n_lat_tiles, i % per_seq, per_seq), 0)
    n_blocks = (DA_Q_W * 2) // HEAD_DIM
    return pl.pallas_call(
        _qk_prep_kernel,
        grid=(geo.m // tm, n_blocks),
        in_specs=[
            pl.BlockSpec((tm, HEAD_DIM), lambda i, j: (i, j)),
            pl.BlockSpec((1, 1, HEAD_DIM), lambda i, j: (j // DA_HEADS, 0, 0)),
            pl.BlockSpec((tm, HEAD_DIM), tab),
            pl.BlockSpec((tm, HEAD_DIM), tab),
        ],
        out_specs=pl.BlockSpec((tm, HEAD_DIM), lambda i, j: (i, j)),
        out_shape=jax.ShapeDtypeStruct((geo.m, 2 * DA_Q_W), BF16),
        compiler_params=_params(("arbitrary", "arbitrary"), 32),
        name="qk_prep",
    )(proj, gains, cos_t, sin_t)


def _attn_kernel(lam_ref, q_ref, kl_ref, kc_ref, vl_ref, vc_ref, g_ref, o_ref, *, tk, n_lat_chunks, nq_lat,
                 sub_scale):
    q = q_ref[...]
    tq = q.shape[0]
    lane = lax.broadcasted_iota(jnp.int32, q.shape, 1)
    zero = jnp.zeros_like(q)
    qa = jnp.where(lane < DA_QK_DIM, q, zero)
    qb = jnp.where(lane < DA_QK_DIM, zero, q)
    nt = (((1,), (1,)), ((), ()))

    def update(s, v, m, l, acc):
        m_new = jnp.maximum(m, jnp.max(s, axis=-1, keepdims=True))
        alpha = jnp.exp(m - m_new)
        p = jnp.exp(s - m_new)
        l_new = alpha * l + jnp.sum(p, axis=-1, keepdims=True)
        acc_new = alpha * acc + jnp.dot(p.astype(BF16), v, preferred_element_type=F32)
        return m_new, l_new, acc_new

    def chunk(k, v, carry):
        ma, la, acca, mb, lb, accb = carry
        sa = lax.dot_general(qa, k, nt, preferred_element_type=F32)
        sb = lax.dot_general(qb, k, nt, preferred_element_type=F32)
        return update(sa, v, ma, la, acca) + update(sb, v, mb, lb, accb)

    def lat_chunk(c, carry):
        r0 = pl.multiple_of(c * tk, tk)
        return chunk(kl_ref[pl.ds(r0, tk), :], vl_ref[pl.ds(r0, tk), :], carry)

    m0 = jnp.full((tq, 1), -jnp.inf, F32)
    l0 = jnp.zeros((tq, 1), F32)
    a0 = jnp.zeros((tq, HEAD_DIM), F32)
    n_chunks = jnp.where(pl.program_id(2) < nq_lat, n_lat_chunks, 0)
    carry = lax.fori_loop(0, n_chunks, lat_chunk, (m0, l0, a0, m0, l0, a0))
    ma, la, acca, mb, lb, accb = chunk(kc_ref[...], vc_ref[...], carry)
    o = acca / la - lam_ref[0, 0] * (accb / lb)
    y = o * lax.rsqrt(jnp.mean(o * o, axis=-1, keepdims=True) + EPS) * g_ref[...]
    o_ref[...] = (y * sub_scale).astype(BF16)


def _diff_attention(qk, proj, geo, lam, subln_g, lam_init):
    tq = _pow2_tile(256, geo.s, geo.c)
    tk = _pow2_tile(512, geo.s)
    nq_lat, nq_ctx = geo.s // tq, geo.c // tq
    k_col0, v_col0 = DA_Q_W // HEAD_DIM, 2 * DA_Q_W // HEAD_DIM
    ctx_blk0 = geo.n_lat // geo.c

    def q_idx(b, h, qi, col0=0):
        lat = b * nq_lat + qi
        ctx = geo.n_lat // tq + b * nq_ctx + (qi - nq_lat)
        return (jnp.where(qi < nq_lat, lat, ctx), col0 + h)

    kernel = functools.partial(_attn_kernel, tk=tk, n_lat_chunks=geo.s // tk, nq_lat=nq_lat,
                               sub_scale=1.0 - lam_init)
    return pl.pallas_call(
        kernel,
        grid=(geo.b, DA_HEADS, nq_lat + nq_ctx),
        in_specs=[
            pl.BlockSpec(memory_space=pltpu.SMEM),
            pl.BlockSpec((tq, HEAD_DIM), q_idx),
            pl.BlockSpec((geo.s, HEAD_DIM), lambda b, h, qi: (b, k_col0 + h)),
            pl.BlockSpec((geo.c, HEAD_DIM), lambda b, h, qi: (ctx_blk0 + b, k_col0 + h)),
            pl.BlockSpec((geo.s, HEAD_DIM), lambda b, h, qi: (b, v_col0 + h)),
            pl.BlockSpec((geo.c, HEAD_DIM), lambda b, h, qi: (ctx_blk0 + b, v_col0 + h)),
            pl.BlockSpec((1, HEAD_DIM), lambda b, h, qi: (0, 0)),
        ],
        out_specs=pl.BlockSpec((tq, HEAD_DIM), q_idx),
        out_shape=jax.ShapeDtypeStruct((geo.m, DA_V_W), BF16),
        compiler_params=_params(("arbitrary", "arbitrary", "arbitrary"), 32),
        name="diff_attn",
    )(lam.reshape(1, 1), qk, qk, qk, proj, proj, subln_g.reshape(1, HEAD_DIM).astype(F32))


def _gelu(x):
    return 0.5 * x * (1.0 + lax.erf(x * (1.0 / math.sqrt(2.0))))


def _gmlp_kernel(u_ref, v_ref, lng_ref, lnb_ref, ws_ref, bs_ref, o_ref):
    tm = u_ref.shape[0]
    for r in range(tm // GM_CHUNK):
        rows = slice(r * GM_CHUNK, (r + 1) * GM_CHUNK)
        for g in range(GM_GROUPS):
            cols = slice(g * HEAD_DIM, (g + 1) * HEAD_DIM)
            v = _gelu(v_ref[rows, cols].astype(F32))
            mu = jnp.mean(v, axis=-1, keepdims=True)
            var = jnp.mean(jnp.square(v - mu), axis=-1, keepdims=True)
            vn = (v - mu) * lax.rsqrt(var + EPS) * lng_ref[:, cols] + lnb_ref[:, cols]
            s = jnp.dot(ws_ref[g], vn.astype(BF16), preferred_element_type=F32) + bs_ref[:, cols]
            o_ref[rows, cols] = (_gelu(u_ref[rows, cols].astype(F32)) * s).astype(BF16)


def _gmlp(proj, geo, ln_g, ln_b, w_s, b_s):
    tm = _pow2_tile(256, geo.s, geo.c)
    u_blk = (2 * DA_Q_W + DA_V_W) // GM_W
    bs_full = jnp.repeat(b_s.astype(F32).T, HEAD_DIM, axis=1)
    return pl.pallas_call(
        _gmlp_kernel,
        grid=(geo.m // tm,),
        in_specs=[
            pl.BlockSpec((tm, GM_W), lambda i: (i, u_blk)),
            pl.BlockSpec((tm, GM_W), lambda i: (i, u_blk + 1)),
            pl.BlockSpec((1, GM_W), lambda i: (0, 0)),
            pl.BlockSpec((1, GM_W), lambda i: (0, 0)),
            pl.BlockSpec((GM_GROUPS, GM_CHUNK, GM_CHUNK), lambda i: (0, 0, 0)),
            pl.BlockSpec((GM_CHUNK, GM_W), lambda i: (0, 0)),
        ],
        out_specs=pl.BlockSpec((tm, GM_W), lambda i: (i, 0)),
        out_shape=jax.ShapeDtypeStruct((geo.m, GM_W), BF16),
        compiler_params=_params(("arbitrary",), 32),
        name="gmlp",
    )(proj, proj, ln_g.reshape(1, GM_W).astype(F32), ln_b.reshape(1, GM_W).astype(F32), w_s.astype(BF16), bs_full)


def _pack_bf16_pair(lo, hi):
    lo_bits = lax.bitcast_convert_type(lo.astype(BF16).astype(F32), U32)
    hi_bits = lax.bitcast_convert_type(hi.astype(BF16).astype(F32), U32)
    return (lo_bits >> 16) | (hi_bits & jnp.uint32(0xFFFF0000))


def _resid_norm_route(y, x_ref, gate_ref, g_ref, sh_ref, sc_ref, rw_ref, rb_ref, xn_ref, hp_ref, lg_ref):
    xn = x_ref[...] + gate_ref[0] * y
    xn_ref[...] = xn
    h = _modulated_norm(xn, g_ref[...], sh_ref[0], sc_ref[0])
    lg_ref[...] = jnp.dot(h, rw_ref[...], preferred_element_type=F32, precision=HIGHEST) + rb_ref[...]
    half = h.shape[1] // 2
    hp_ref[...] = _pack_bf16_pair(h[:, :half], h[:, half:])


def _epilogue_specs(geo, tm, d, grid_rank):
    def ix(f):
        return (lambda i: f(i)) if grid_rank == 1 else (lambda i, k: f(i))
    mod = lambda k: ix(lambda i: (geo.mod_row(i, tm), 0, k))
    in_specs = [
        pl.BlockSpec((tm, d), ix(lambda i: (i, 0))),
        pl.BlockSpec((1, 1, d), mod(2)),
        pl.BlockSpec((1, d), ix(lambda i: (0, 0))),
        pl.BlockSpec((1, 1, d), mod(3)),
        pl.BlockSpec((1, 1, d), mod(4)),
        pl.BlockSpec((d, ROUTER_PAD), ix(lambda i: (0, 0))),
        pl.BlockSpec((1, ROUTER_PAD), ix(lambda i: (0, 0))),
    ]
    out_specs = [
        pl.BlockSpec((tm, d), ix(lambda i: (i, 0))),
        pl.BlockSpec((tm, d // 2), ix(lambda i: (i, 0))),
        pl.BlockSpec((tm, ROUTER_PAD), ix(lambda i: (i, 0))),
    ]
    return in_specs, out_specs


def _epilogue_out_shapes(n_rows, d):
    return [jax.ShapeDtypeStruct((n_rows, d), F32), jax.ShapeDtypeStruct((n_rows, d // 2), U32),
            jax.ShapeDtypeStruct((n_rows, ROUTER_PAD), F32)]


def _router_operands(router_w, router_b):
    d = router_w.shape[0]
    rw = jnp.zeros((d, ROUTER_PAD), F32).at[:, :N_EXPERTS].set(router_w.astype(F32))
    rb = jnp.zeros((1, ROUTER_PAD), F32).at[0, :N_EXPERTS].set(router_b.astype(F32))
    return rw, rb


def _attn_out_kernel(a_ref, gm_ref, wo_ref, x_ref, gate_ref, g_ref, sh_ref, sc_ref, rw_ref, rb_ref,
                     xn_ref, hp_ref, lg_ref):
    y = jnp.dot(a_ref[...], wo_ref[:DA_V_W, :], preferred_element_type=F32)
    y = y + jnp.dot(gm_ref[...], wo_ref[DA_V_W:, :], preferred_element_type=F32)
    _resid_norm_route(y, x_ref, gate_ref, g_ref, sh_ref, sc_ref, rw_ref, rb_ref, xn_ref, hp_ref, lg_ref)


def _attn_out(attn, gm, w_o, xa, geo, mod, g2, rw, rb):
    d = xa.shape[1]
    tm = _pow2_tile(512, geo.s, geo.n_ctx)
    ep_in, ep_out = _epilogue_specs(geo, tm, d, 1)
    return pl.pallas_call(
        _attn_out_kernel,
        grid=(geo.m // tm,),
        in_specs=[
            pl.BlockSpec((tm, DA_V_W), lambda i: (i, 0)),
            pl.BlockSpec((tm, GM_W), lambda i: (i, 0)),
            pl.BlockSpec((DA_V_W + GM_W, d), lambda i: (0, 0)),
        ] + ep_in,
        out_specs=ep_out,
        out_shape=_epilogue_out_shapes(geo.m, d),
        compiler_params=_params(("arbitrary",), 52),
        name="attn_out",
    )(attn, gm, w_o, xa, mod, g2.reshape(1, d), mod, mod, rw, rb)


def _route(logits, n_tok):
    top_val, top_idx = lax.top_k(logits[:, :N_EXPERTS], TOP_K)
    gates = jax.nn.softmax(top_val, axis=-1)
    n_asg = n_tok * TOP_K
    n_tiles = n_asg // MOE_TM + N_EXPERTS
    flat_e = top_idx.reshape(-1).astype(jnp.int32)
    onehot = (flat_e[:, None] == jnp.arange(N_EXPERTS, dtype=jnp.int32)[None, :]).astype(jnp.int32)
    csum = jnp.cumsum(onehot, axis=0)
    rank = jnp.take_along_axis(csum, flat_e[:, None], axis=1)[:, 0] - 1
    counts = csum[-1]
    tiles_e = (counts + MOE_TM - 1) // MOE_TM
    tile_end = jnp.cumsum(tiles_e)
    tile_start = tile_end - tiles_e
    n_valid = tile_end[-1]
    pos = (tile_start[flat_e] + rank // MOE_TM) * MOE_TM + rank % MOE_TM
    flat_tok = jnp.repeat(jnp.arange(n_tok, dtype=jnp.int32), TOP_K)
    row_tok = jnp.zeros((n_tiles * MOE_TM,), jnp.int32).at[pos].set(flat_tok).reshape(n_tiles, 1, MOE_TM)
    t = jnp.arange(n_tiles, dtype=jnp.int32)
    t_eff = jnp.minimum(t, n_valid - 1)
    tile_e = jnp.minimum(jnp.searchsorted(tile_end, t_eff, side="right"), N_EXPERTS - 1).astype(jnp.int32)
    rows = jnp.clip(counts[tile_e] - (t_eff - tile_start[tile_e]) * MOE_TM, 0, MOE_TM)
    tile_nrb = jnp.where(t < n_valid, (rows + MOE_RB - 1) // MOE_RB, 0).astype(jnp.int32)
    return tile_e, t, tile_nrb, row_tok, pos.reshape(n_tok, TOP_K).astype(jnp.int32), gates


def _moe_kernel(te_ref, tb_ref, tn_ref, idx_ref, idx_next_ref, hp_hbm, w1g_ref, w1l_ref, b1g_ref, b1l_ref,
                w2_ref, b2_ref, o_ref, xbuf, xs, sem):
    t, f = pl.program_id(0), pl.program_id(1)
    nrb = tn_ref[t]
    slot = t % 2
    half = xbuf.shape[2]

    def gather_rows(rows_ref, n_rows, slot_):
        def body(r, _):
            tok = rows_ref[0, 0, r]
            pltpu.make_async_copy(hp_hbm.at[pl.ds(tok, 1), :], xbuf.at[slot_, pl.ds(r, 1), :], sem.at[slot_]).start()
            return 0
        lax.fori_loop(0, n_rows, body, 0)

    @pl.when(f == 0)
    def _():
        @pl.when(t == 0)
        def _():
            gather_rows(idx_ref, nrb * MOE_RB, slot)

        @pl.when(t + 1 < pl.num_programs(0))
        def _():
            gather_rows(idx_next_ref, tn_ref[t + 1] * MOE_RB, 1 - slot)

        def wait_block(i, _):
            r0 = pl.multiple_of(i * MOE_RB, MOE_RB)
            pltpu.make_async_copy(hp_hbm.at[pl.ds(0, MOE_RB), :], xbuf.at[slot, pl.ds(r0, MOE_RB), :],
                                  sem.at[slot]).wait()
            return 0
        lax.fori_loop(0, nrb, wait_block, 0)

        def unpack_block(i, _):
            r0 = pl.multiple_of(i * MOE_RB, MOE_RB)
            u = xbuf[slot, pl.ds(r0, MOE_RB), :]
            xs[pl.ds(r0, MOE_RB), :half] = lax.bitcast_convert_type(u << 16, F32).astype(BF16)
            xs[pl.ds(r0, MOE_RB), half:] = lax.bitcast_convert_type(u & jnp.uint32(0xFFFF0000), F32).astype(BF16)
            return 0
        lax.fori_loop(0, nrb, unpack_block, 0)

        def zero_block(i, _):
            r0 = pl.multiple_of(i * MOE_RB, MOE_RB)
            o_ref[pl.ds(r0, MOE_RB), :] = jnp.zeros((MOE_RB, o_ref.shape[1]), F32)
            return 0
        lax.fori_loop(nrb, MOE_TM // MOE_RB, zero_block, 0)

    @pl.when(nrb > 0)
    def _():
        w1g = w1g_ref[0, 0].astype(BF16)
        w1l = w1l_ref[0, 0].astype(BF16)
        w2 = w2_ref[0, 0].astype(BF16)

        def block(i, _):
            r0 = pl.multiple_of(i * MOE_RB, MOE_RB)
            x = xs[pl.ds(r0, MOE_RB), :]
            g = jnp.minimum(jnp.dot(x, w1g, preferred_element_type=F32) + b1g_ref[0, 0], SWIGLU_LIMIT)
            lin = jnp.clip(jnp.dot(x, w1l, preferred_element_type=F32) + b1l_ref[0, 0], -SWIGLU_LIMIT, SWIGLU_LIMIT)
            act = g * jax.nn.sigmoid(SWIGLU_ALPHA * g) * (lin + 1.0)
            y = jnp.dot(act.astype(BF16), w2, preferred_element_type=F32)

            @pl.when(f == 0)
            def _():
                o_ref[pl.ds(r0, MOE_RB), :] = y + b2_ref[0, 0]

            @pl.when(f > 0)
            def _():
                o_ref[pl.ds(r0, MOE_RB), :] += y
            return 0
        lax.fori_loop(0, nrb, block, 0)


def _moe_experts(hp, tile_e, tile_blk, tile_nrb, row_tok, w1, b1, w2, b2, li):
    n_tiles = row_tok.shape[0]
    d = w2.shape[-1]
    nf = D_EXPERT // MOE_TF
    fz = lambda f, tn, t: jnp.where(tn[t] > 0, f, nf - 1)
    b1r = b1.reshape(b1.shape[0], N_EXPERTS, 1, 2 * D_EXPERT)
    b2r = b2.reshape(b2.shape[0], N_EXPERTS, 1, d)
    grid_spec = pltpu.PrefetchScalarGridSpec(
        num_scalar_prefetch=3,
        grid=(n_tiles, nf),
        in_specs=[
            pl.BlockSpec((1, 1, MOE_TM), lambda t, f, te, tb, tn: (t, 0, 0), memory_space=pltpu.SMEM),
            pl.BlockSpec((1, 1, MOE_TM), lambda t, f, te, tb, tn: (jnp.minimum(t + 1, n_tiles - 1), 0, 0),
                         memory_space=pltpu.SMEM),
            pl.BlockSpec(memory_space=pl.ANY),
            pl.BlockSpec((1, 1, d, MOE_TF), lambda t, f, te, tb, tn: (li, te[t], 0, fz(f, tn, t))),
            pl.BlockSpec((1, 1, d, MOE_TF), lambda t, f, te, tb, tn: (li, te[t], 0, nf + fz(f, tn, t))),
            pl.BlockSpec((1, 1, 1, MOE_TF), lambda t, f, te, tb, tn: (li, te[t], 0, fz(f, tn, t))),
            pl.BlockSpec((1, 1, 1, MOE_TF), lambda t, f, te, tb, tn: (li, te[t], 0, nf + fz(f, tn, t))),
            pl.BlockSpec((1, 1, MOE_TF, d), lambda t, f, te, tb, tn: (li, te[t], fz(f, tn, t), 0)),
            pl.BlockSpec((1, 1, 1, d), lambda t, f, te, tb, tn: (li, te[t], 0, 0)),
        ],
        out_specs=pl.BlockSpec((MOE_TM, d), lambda t, f, te, tb, tn: (tb[t], 0)),
        scratch_shapes=[
            pltpu.VMEM((2, MOE_TM, d // 2), U32),
            pltpu.VMEM((MOE_TM, d), BF16),
            pltpu.SemaphoreType.DMA((2,)),
        ],
    )
    return pl.pallas_call(
        _moe_kernel,
        grid_spec=grid_spec,
        out_shape=jax.ShapeDtypeStruct((n_tiles * MOE_TM, d), F32),
        compiler_params=_params(("arbitrary", "arbitrary"), 56),
        name="moe_experts",
    )(tile_e, tile_blk, tile_nrb, row_tok, row_tok, hp, w1, w1, b1r, b1r, w2, b2r)


def _combine_kernel(pos_ref, pos_next_ref, ys_hbm, gate_ref, x_ref, g5_ref, o_ref, buf, sem):
    i = pl.program_id(0)
    slot = i % 2
    n_rows = buf.shape[1]
    tc = n_rows // TOP_K

    def gather_rows(rows_ref, slot_):
        def body(r, _):
            row = rows_ref[0, 0, r]
            pltpu.make_async_copy(ys_hbm.at[pl.ds(row, 1), :], buf.at[slot_, pl.ds(r, 1), :], sem.at[slot_]).start()
            return 0
        lax.fori_loop(0, n_rows, body, 0)

    @pl.when(i == 0)
    def _():
        gather_rows(pos_ref, slot)

    @pl.when(i + 1 < pl.num_programs(0))
    def _():
        gather_rows(pos_next_ref, 1 - slot)

    pltpu.make_async_copy(ys_hbm.at[pl.ds(0, n_rows), :], buf.at[slot], sem.at[slot]).wait()
    gate = gate_ref[...]
    acc = gate[:, 0:1] * buf[slot, 0:tc, :]
    for k in range(1, TOP_K):
        acc = acc + gate[:, k:k + 1] * buf[slot, k * tc:(k + 1) * tc, :]
    o_ref[...] = x_ref[...] + g5_ref[0] * acc


def _moe_combine(ys, pos, gates, xn, geo, n_tok, mod):
    d = xn.shape[1]
    tc = _pow2_tile(COMBINE_TC, geo.s, geo.n_ctx)
    n_steps = n_tok // tc
    pos_t = pos.reshape(n_steps, tc, TOP_K).transpose(0, 2, 1).reshape(n_steps, 1, TOP_K * tc)
    return pl.pallas_call(
        _combine_kernel,
        grid=(n_steps,),
        in_specs=[
            pl.BlockSpec((1, 1, TOP_K * tc), lambda i: (i, 0, 0), memory_space=pltpu.SMEM),
            pl.BlockSpec((1, 1, TOP_K * tc), lambda i: (jnp.minimum(i + 1, n_steps - 1), 0, 0),
                         memory_space=pltpu.SMEM),
            pl.BlockSpec(memory_space=pl.ANY),
            pl.BlockSpec((tc, TOP_K), lambda i: (i, 0)),
            pl.BlockSpec((tc, d), lambda i: (i, 0)),
            pl.BlockSpec((1, 1, d), lambda i: (geo.mod_row(i, tc), 0, 5)),
        ],
        out_specs=pl.BlockSpec((tc, d), lambda i: (i, 0)),
        out_shape=jax.ShapeDtypeStruct((n_tok, d), F32),
        scratch_shapes=[pltpu.VMEM((2, TOP_K * tc, d), F32), pltpu.SemaphoreType.DMA((2,))],
        compiler_params=_params(("arbitrary",), 40),
        name="moe_combine",
    )(pos_t, pos_t, ys, gates, xn, mod)


def _moe_layer(xn, hp, logits, geo, n_tok, mod, w1, b1, w2, b2, li):
    tile_e, tile_blk, tile_nrb, row_tok, pos, gates = _route(logits[:n_tok], n_tok)
    ys = _moe_experts(hp, tile_e, tile_blk, tile_nrb, row_tok, w1, b1, w2, b2, li)
    return _moe_combine(ys, pos, gates, xn, geo, n_tok, mod)


def _ssd_prep_kernel(raw_ref, bias_ref, aneg_ref, dt_ref, ac_ref):
    tm = raw_ref.shape[0]
    li = lax.broadcasted_iota(jnp.int32, (SSD_CHUNK, SSD_CHUNK), 0)
    si = lax.broadcasted_iota(jnp.int32, (SSD_CHUNK, SSD_CHUNK), 1)
    tri_f = (si <= li).astype(F32)
    tri_b = (si >= li).astype(F32)
    lane = lax.broadcasted_iota(jnp.int32, (SSD_CHUNK, SSD_DT_W), 1)
    for c in range(tm // SSD_CHUNK):
        rows = slice(c * SSD_CHUNK, (c + 1) * SSD_CHUNK)
        v = raw_ref[rows, :] + bias_ref[...]
        dt = jnp.maximum(v, 0.0) + jnp.log1p(jnp.exp(-jnp.abs(v)))
        a = dt * aneg_ref[...]
        fwd = jnp.dot(tri_f, a, preferred_element_type=F32, precision=HIGHEST)
        bwd = jnp.dot(tri_b, a, preferred_element_type=F32, precision=HIGHEST)
        dt_ref[rows, :] = dt
        ac_ref[rows, :] = jnp.where(lane < SSD_HEADS, fwd, bwd)


def _ssd_prep(dt_raw, geo, dt_bias, a_log):
    tm = _pow2_tile(512, geo.s, geo.c)
    a_neg = -jnp.exp(a_log.astype(F32)).reshape(1, SSD_DT_W)
    spec = pl.BlockSpec((tm, SSD_DT_W), lambda i: (i, 0))
    vec = pl.BlockSpec((1, SSD_DT_W), lambda i: (0, 0))
    return pl.pallas_call(
        _ssd_prep_kernel,
        grid=(geo.m // tm,),
        in_specs=[spec, vec, vec],
        out_specs=[spec, spec],
        out_shape=[jax.ShapeDtypeStruct((geo.m, SSD_DT_W), F32)] * 2,
        compiler_params=_params(("arbitrary",), 32),
        name="ssd_prep",
    )(dt_raw, dt_bias.astype(F32).reshape(1, SSD_DT_W), a_neg)


CONV_HALO = 16


def _conv_kernel(prev_ref, cur_ref, next_ref, w_ref, b_ref, o_ref, ext, *, geo, tm):
    i = pl.program_id(0)
    n_lat_tiles = geo.n_lat // tm
    per_lat, per_ctx = geo.s // tm, geo.c // tm
    lat = i < n_lat_tiles
    p = jnp.where(lat, i % per_lat, (i - n_lat_tiles) % per_ctx)
    first = p == 0
    last = jnp.where(lat, p == per_lat - 1, p == per_ctx - 1)
    ext[0:CONV_HALO, :] = jnp.where(first, 0.0, prev_ref[...].astype(F32))
    ext[CONV_HALO:CONV_HALO + tm, :] = cur_ref[...].astype(F32)
    ext[CONV_HALO + tm:, :] = jnp.where(last, 0.0, next_ref[...].astype(F32))
    pad = SSD_CONV_W // 2
    acc = w_ref[0:1, :] * ext[pl.ds(CONV_HALO - pad, tm), :]
    for k in range(1, SSD_CONV_W):
        acc = acc + w_ref[k:k + 1, :] * ext[pl.ds(CONV_HALO - pad + k, tm), :]
    acc = acc + b_ref[...]
    o_ref[...] = (acc * jax.nn.sigmoid(acc)).astype(BF16)


def _conv_silu(zx, geo, conv_w, conv_b):
    tm = _pow2_tile(256, geo.s, geo.c)
    tc = 1024
    col0 = SSD_D_INNER // tc
    hb = tm // CONV_HALO
    n_halo_blocks = geo.m // CONV_HALO
    return pl.pallas_call(
        functools.partial(_conv_kernel, geo=geo, tm=tm),
        grid=(geo.m // tm, SSD_CONV_CH // tc),
        in_specs=[
            pl.BlockSpec((CONV_HALO, tc), lambda i, j: (jnp.maximum(i * hb - 1, 0), col0 + j)),
            pl.BlockSpec((tm, tc), lambda i, j: (i, col0 + j)),
            pl.BlockSpec((CONV_HALO, tc), lambda i, j: (jnp.minimum((i + 1) * hb, n_halo_blocks - 1), col0 + j)),
            pl.BlockSpec((SSD_CONV_W, tc), lambda i, j: (0, j)),
            pl.BlockSpec((1, tc), lambda i, j: (0, j)),
        ],
        out_specs=pl.BlockSpec((tm, tc), lambda i, j: (i, j)),
        out_shape=jax.ShapeDtypeStruct((geo.m, SSD_CONV_CH), BF16),
        scratch_shapes=[pltpu.VMEM((tm + 2 * CONV_HALO, tc), F32)],
        compiler_params=_params(("arbitrary", "arbitrary"), 32),
        name="ssd_conv",
    )(zx, zx, zx, conv_w.astype(F32), conv_b.astype(F32).reshape(1, SSD_CONV_CH))


def _ssd_scan_kernel(xs_ref, b_ref, c_ref, dt_ref, ac_ref, y_ref, state, xg, bt_g, c_g, dt_g, ac_g, ac_t, y_g):
    d, s = pl.program_id(1), pl.program_id(2)
    q = SSD_CHUNK
    is_fwd = d == 0

    @pl.when(s == 0)
    def _():
        state[...] = jnp.zeros(state.shape, F32)

    dt2, ac2 = dt_ref[...], ac_ref[...]
    dtc = jnp.where(is_fwd, dt2[:, :SSD_HEADS], dt2[:, SSD_HEADS:])
    acc = jnp.where(is_fwd, ac2[:, :SSD_HEADS], ac2[:, SSD_HEADS:])
    ac_t[...] = ac2.T
    b_f32 = b_ref[...].astype(F32)
    for g in range(SSD_GROUPS):
        xg[g] = xs_ref[:, g * SSD_GROUP_W:(g + 1) * SSD_GROUP_W]
        bt_g[g] = b_f32[:, g * SSD_STATE:(g + 1) * SSD_STATE].T.astype(BF16)
        c_g[g] = c_ref[:, g * SSD_STATE:(g + 1) * SSD_STATE]
        dt_g[g] = dtc[:, g * SSD_HPG:(g + 1) * SSD_HPG]
        ac_g[g] = acc[:, g * SSD_HPG:(g + 1) * SSD_HPG]

    li = lax.broadcasted_iota(jnp.int32, (q, q), 0)
    si = lax.broadcasted_iota(jnp.int32, (q, q), 1)
    mask = jnp.where(is_fwd, li - si, si - li) >= 0
    lane = lax.broadcasted_iota(jnp.int32, (q, LANES), 1)
    lo_half = lane < SSD_HEAD_DIM
    lo_row = lo_half[0:1, :]

    def group(g, _):
        cg = c_g[g]
        cb = jnp.dot(cg, bt_g[g], preferred_element_type=F32)
        dtg, acg = dt_g[g], ac_g[g]
        xf = xg[g].astype(F32)
        a_cols, d_cols, decays = [], [], []
        for j in range(SSD_HPG):
            a_col = jnp.broadcast_to(acg[:, j:j + 1], (q, q))
            a_row = ac_t[pl.ds(d * SSD_HEADS + g * SSD_HPG + j, 1), :]
            seg = a_col - a_row
            decays.append((cb * jnp.exp(jnp.where(mask, seg, -jnp.inf))).astype(BF16))
            a_cols.append(a_col)
            d_cols.append(jnp.broadcast_to(dtg[:, j:j + 1], (q, q)))
        for p in range(SSD_HPG // 2):
            cols = slice(p * LANES, (p + 1) * LANES)
            acx = jnp.where(lo_half, a_cols[2 * p], a_cols[2 * p + 1])
            dtx = jnp.where(lo_half, d_cols[2 * p], d_cols[2 * p + 1])
            a_last = jnp.where(is_fwd, acx[q - 1:q, :], acx[0:1, :])
            xdt = xf[:, cols] * dtx
            xdt_b = xdt.astype(BF16)
            zero = jnp.zeros_like(xdt_b)
            rhs = jnp.concatenate([jnp.where(lo_half, xdt_b, zero), jnp.where(lo_half, zero, xdt_b)], axis=0)
            lhs = jnp.concatenate([decays[2 * p], decays[2 * p + 1]], axis=1)
            st = state[g, :, cols]
            y = jnp.dot(lhs, rhs, preferred_element_type=F32)
            y = y + jnp.dot(cg, st.astype(BF16), preferred_element_type=F32) * jnp.exp(acx)
            y_g[g, :, cols] = y
            wx = (xdt * jnp.exp(a_last - acx)).astype(BF16)
            state[g, :, cols] = st * jnp.exp(a_last) + jnp.dot(bt_g[g], wx, preferred_element_type=F32)
        return 0

    lax.fori_loop(0, SSD_GROUPS, group, 0)
    for g in range(SSD_GROUPS):
        y_ref[0, :, g * SSD_GROUP_W:(g + 1) * SSD_GROUP_W] = y_g[g].astype(BF16)


def _ssd_scan(cv, dt, ac, geo):
    q = SSD_CHUNK
    ncc, nlc = geo.c // q, geo.s // q
    bc_col0 = SSD_D_INNER // (SSD_GROUPS * SSD_STATE)

    def lat_chunk(d, s):
        c = jnp.maximum(s - ncc, 0)
        return jnp.where(d == 0, c, nlc - 1 - c)

    def row_blk(b, d, s):
        ctx_c = jnp.where(d == 0, s, ncc - 1 - s)
        return jnp.where(s < ncc, geo.b * nlc + b * ncc + ctx_c, b * nlc + lat_chunk(d, s))

    return pl.pallas_call(
        _ssd_scan_kernel,
        grid=(geo.b, 2, ncc + nlc),
        in_specs=[
            pl.BlockSpec((q, SSD_D_INNER), lambda b, d, s: (row_blk(b, d, s), 0)),
            pl.BlockSpec((q, SSD_GROUPS * SSD_STATE), lambda b, d, s: (row_blk(b, d, s), bc_col0 + 2 * d)),
            pl.BlockSpec((q, SSD_GROUPS * SSD_STATE), lambda b, d, s: (row_blk(b, d, s), bc_col0 + 2 * d + 1)),
            pl.BlockSpec((q, SSD_DT_W), lambda b, d, s: (row_blk(b, d, s), 0)),
            pl.BlockSpec((q, SSD_DT_W), lambda b, d, s: (row_blk(b, d, s), 0)),
        ],
        out_specs=pl.BlockSpec((1, q, SSD_D_INNER), lambda b, d, s: (d, b * nlc + lat_chunk(d, s), 0)),
        out_shape=jax.ShapeDtypeStruct((2, geo.n_lat, SSD_D_INNER), BF16),
        scratch_shapes=[
            pltpu.VMEM((SSD_GROUPS, SSD_STATE, SSD_GROUP_W), F32),
            pltpu.VMEM((SSD_GROUPS, q, SSD_GROUP_W), BF16),
            pltpu.VMEM((SSD_GROUPS, SSD_STATE, q), BF16),
            pltpu.VMEM((SSD_GROUPS, q, SSD_STATE), BF16),
            pltpu.VMEM((SSD_GROUPS, q, SSD_HPG), F32),
            pltpu.VMEM((SSD_GROUPS, q, SSD_HPG), F32),
            pltpu.VMEM((SSD_DT_W, q), F32),
            pltpu.VMEM((SSD_GROUPS, q, SSD_GROUP_W), F32),
        ],
        compiler_params=_params(("arbitrary", "arbitrary", "arbitrary"), 40),
        name="ssd_scan",
    )(cv, cv, cv, dt, ac)


def _ssd_out_kernel(yf_ref, yb_ref, xs_ref, z_ref, skip_ref, ng_ref, wo_ref, x_ref, gate_ref, g_ref, sh_ref,
                    sc_ref, rw_ref, rb_ref, xn_ref, hp_ref, lg_ref, acc):
    k = pl.program_id(1)
    y = yf_ref[0].astype(F32) + yb_ref[0].astype(F32) + skip_ref[...] * xs_ref[...].astype(F32)
    z = z_ref[...].astype(F32)
    t = y * (z * jax.nn.sigmoid(z))
    parts = []
    for gg in range(t.shape[1] // SSD_GROUP_W):
        tg = t[:, gg * SSD_GROUP_W:(gg + 1) * SSD_GROUP_W]
        parts.append(tg * lax.rsqrt(jnp.mean(tg * tg, axis=-1, keepdims=True) + EPS))
    tn = (jnp.concatenate(parts, axis=1) * ng_ref[...]).astype(BF16)
    contrib = jnp.dot(tn, wo_ref[...], preferred_element_type=F32)

    @pl.when(k == 0)
    def _():
        acc[...] = contrib

    @pl.when(k > 0)
    def _():
        acc[...] += contrib

    @pl.when(k == pl.num_programs(1) - 1)
    def _():
        _resid_norm_route(acc[...], x_ref, gate_ref, g_ref, sh_ref, sc_ref, rw_ref, rb_ref, xn_ref, hp_ref, lg_ref)


def _ssd_out(yd, cv, zx, skip_x, norm_g, w_o, xa, geo, mod, g2, rw, rb):
    d = xa.shape[1]
    tm = _pow2_tile(512, geo.s)
    tk = 1024
    ep_in, ep_out = _epilogue_specs(geo, tm, d, 2)
    return pl.pallas_call(
        _ssd_out_kernel,
        grid=(geo.n_lat // tm, SSD_D_INNER // tk),
        in_specs=[
            pl.BlockSpec((1, tm, tk), lambda i, k: (0, i, k)),
            pl.BlockSpec((1, tm, tk), lambda i, k: (1, i, k)),
            pl.BlockSpec((tm, tk), lambda i, k: (i, k)),
            pl.BlockSpec((tm, tk), lambda i, k: (i, k)),
            pl.BlockSpec((1, tk), lambda i, k: (0, k)),
            pl.BlockSpec((1, tk), lambda i, k: (0, k)),
            pl.BlockSpec((tk, d), lambda i, k: (k, 0)),
        ] + ep_in,
        out_specs=ep_out,
        out_shape=_epilogue_out_shapes(geo.n_lat, d),
        scratch_shapes=[pltpu.VMEM((tm, d), F32)],
        compiler_params=_params(("arbitrary", "arbitrary"), 52),
        name="ssd_out",
    )(yd, yd, cv, zx, skip_x, norm_g.astype(F32).reshape(1, SSD_D_INNER), w_o, xa, mod, g2.reshape(1, d), mod, mod,
      rw, rb)


def kernel(x, c, ctx, c_ctx, ada_w, ada_b, norm_g, attn_w_in, attn_w_o, da_q_norm_g, da_k_norm_g, da_lambda,
           da_subln_g, gm_ln_g, gm_ln_b, gm_w_s, gm_b_s, ssd_w_in, ssd_conv_w, ssd_conv_b, ssd_dt_bias, ssd_a_log,
           ssd_d_skip, ssd_norm_g, ssd_w_o, router_w, router_b, moe_w1, moe_b1, moe_w2, moe_b2):
    bsz, n, d = x.shape
    geo = _Geo(bsz, n, ctx.shape[1])
    assert ada_w.shape[0] == 2 and d == D_MODEL, "even (attention/gMLP) layer followed by a final odd (SSD) layer"

    n_rows = -(-(bsz + 1) // 8) * 8
    cond = jnp.zeros((n_rows, d), F32).at[:bsz].set(c).at[bsz].set(c_ctx)
    mods = _adaln_mods(cond, ada_w, ada_b)
    xa = jnp.concatenate([x.reshape(geo.n_lat, d), ctx.reshape(geo.n_ctx, d)], axis=0)

    mod = mods[0].reshape(n_rows, 1, 6 * d)
    proj = _norm_proj(xa, geo, norm_g[0, 0], mod, attn_w_in[0].astype(BF16), BF16, 1024)
    qk = _qk_prep(proj, geo, da_q_norm_g[0], da_k_norm_g[0])
    lam_init = 0.8 - 0.6 * math.exp(-0.3 * 0)
    lv = da_lambda[0].astype(F32)
    lam = jnp.exp(jnp.sum(lv[0] * lv[1])) - jnp.exp(jnp.sum(lv[2] * lv[3])) + lam_init
    attn = _diff_attention(qk, proj, geo, lam, da_subln_g[0], lam_init)
    gm = _gmlp(proj, geo, gm_ln_g[0], gm_ln_b[0], gm_w_s[0], gm_b_s[0])
    rw, rb = _router_operands(router_w[0], router_b[0])
    xn, hp, logits = _attn_out(attn, gm, attn_w_o[0].astype(BF16), xa, geo, mod, norm_g[0, 1], rw, rb)
    xa = _moe_layer(xn, hp, logits, geo, geo.m, mod, moe_w1, moe_b1, moe_w2, moe_b2, 0)

    mod = mods[1].reshape(n_rows, 1, 6 * d)
    w_in = ssd_w_in[0].astype(BF16)
    n_zx = SSD_D_INNER + SSD_CONV_CH
    zx = _norm_proj(xa, geo, norm_g[1, 0], mod, w_in[:, :n_zx], BF16, 1024)
    dt_raw = _norm_proj(xa, geo, norm_g[1, 0], mod, w_in[:, n_zx:], F32, SSD_DT_W)
    dt, ac = _ssd_prep(dt_raw, geo, ssd_dt_bias[0], ssd_a_log[0])
    cv = _conv_silu(zx, geo, ssd_conv_w[0], ssd_conv_b[0])
    yd = _ssd_scan(cv, dt, ac, geo)
    skip_x = jnp.repeat(ssd_d_skip[0].astype(F32).reshape(-1), SSD_HEAD_DIM).reshape(1, SSD_D_INNER)
    rw, rb = _router_operands(router_w[1], router_b[1])
    xn, hp, logits = _ssd_out(yd, cv, zx, skip_x, ssd_norm_g[0], ssd_w_o[0].astype(BF16), xa, geo, mod,
                              norm_g[1, 1], rw, rb)
    out = _moe_layer(xn, hp, logits, geo, geo.n_lat, mod, moe_w1, moe_b1, moe_w2, moe_b2, 1)
    return out.reshape(bsz, n, d)
```

```python
import functools
import math

import jax
import jax.numpy as jnp
from jax import lax
from jax.experimental import pallas as pl
from jax.experimental.pallas import tpu as pltpu

F32 = jnp.float32
BF16 = jnp.bfloat16
U32 = jnp.uint32
HIGHEST = lax.Precision.HIGHEST

D_MODEL = 2048
GRID_W = 64
EPS = 1e-6
LANES = 128
HEAD_DIM = 128
DA_HEADS = 8
DA_QK_DIM = 64
ROPE_BASE = 10000.0
ROPE_AXIS_DIM = DA_QK_DIM // 2
ROPE_FREQS = ROPE_AXIS_DIM // 2
GM_GROUPS = 8
GM_CHUNK = 128
DA_Q_W = DA_HEADS * 2 * DA_QK_DIM
DA_V_W = DA_HEADS * HEAD_DIM
GM_W = GM_GROUPS * HEAD_DIM
EVEN_IN = 2 * DA_Q_W + DA_V_W + 2 * GM_W
SSD_D_INNER = 2 * D_MODEL
SSD_HEAD_DIM = 64
SSD_HEADS = SSD_D_INNER // SSD_HEAD_DIM
SSD_GROUPS = 8
SSD_HPG = SSD_HEADS // SSD_GROUPS
SSD_GROUP_W = SSD_HPG * SSD_HEAD_DIM
SSD_STATE = 128
SSD_CONV_W = 7
SSD_CHUNK = 128
SSD_BC_W = 2 * 2 * SSD_GROUPS * SSD_STATE
SSD_CONV_CH = SSD_D_INNER + SSD_BC_W
SSD_DT_W = 2 * SSD_HEADS
N_EXPERTS = 32
TOP_K = 4
D_EXPERT = D_MODEL
SWIGLU_LIMIT = 7.0
SWIGLU_ALPHA = 1.702

MOE_TM = 2304
MOE_RB = 256
MOE_TF = 256
MOE_ISSUE_UNROLL = 8
COMBINE_TC = 512
ROUTER_PAD = 128
MIB = 2 ** 20
V7X_VMEM_BYTES = 64 * MIB


def _params(sem, vmem_mib):
    assert vmem_mib * MIB < V7X_VMEM_BYTES
    return pltpu.CompilerParams(dimension_semantics=sem, vmem_limit_bytes=vmem_mib * MIB)


def _pow2_tile(limit, *sizes):
    t = limit
    while any(s % t for s in sizes):
        t //= 2
    return t


class _Geo:
    def __init__(self, b, s, c):
        self.b, self.s, self.c = b, s, c
        self.n_lat, self.n_ctx = b * s, b * c
        self.m = self.n_lat + self.n_ctx

    def mod_row(self, i, tm):
        return jnp.where(i < self.n_lat // tm, i // (self.s // tm), self.b)


def _adaln_kernel(c_ref, w_ref, b_ref, o_ref):
    c = c_ref[...]
    s = c * jax.nn.sigmoid(c)
    o_ref[0] = jnp.dot(s.astype(BF16), w_ref[0].astype(BF16), preferred_element_type=F32) + b_ref[0]


def _adaln_mods(cond, ada_w, ada_b):
    n_l, d, n = ada_w.shape
    r = cond.shape[0]
    tn = 1024
    return pl.pallas_call(
        _adaln_kernel,
        grid=(n_l, n // tn),
        in_specs=[
            pl.BlockSpec((r, d), lambda l, j: (0, 0)),
            pl.BlockSpec((1, d, tn), lambda l, j: (l, 0, j)),
            pl.BlockSpec((1, 1, tn), lambda l, j: (l, 0, j)),
        ],
        out_specs=pl.BlockSpec((1, r, tn), lambda l, j: (l, 0, j)),
        out_shape=jax.ShapeDtypeStruct((n_l, r, n), F32),
        compiler_params=_params(("arbitrary", "arbitrary"), 40),
        name="adaln",
    )(cond, ada_w, ada_b.reshape(n_l, 1, n))


def _modulated_norm(x, g, shift, scale):
    y = x * lax.rsqrt(jnp.mean(x * x, axis=-1, keepdims=True) + EPS)
    return y * g * (1.0 + scale) + shift


def _norm_proj_kernel(x_ref, g_ref, sh_ref, sc_ref, w_ref, o_ref, h_scr):
    @pl.when(pl.program_id(1) == 0)
    def _():
        h_scr[...] = _modulated_norm(x_ref[...], g_ref[...], sh_ref[0], sc_ref[0]).astype(BF16)

    o_ref[...] = jnp.dot(h_scr[...], w_ref[...], preferred_element_type=F32).astype(o_ref.dtype)


def _norm_proj(xa, geo, g, mod, w, out_dtype, tn):
    d = xa.shape[1]
    n = w.shape[1]
    tm = _pow2_tile(1024, geo.s, geo.n_ctx)
    row = lambda i, j: (geo.mod_row(i, tm), 0, 0)
    row_scale = lambda i, j: (geo.mod_row(i, tm), 0, 1)
    return pl.pallas_call(
        _norm_proj_kernel,
        grid=(geo.m // tm, n // tn),
        in_specs=[
            pl.BlockSpec((tm, d), lambda i, j: (i, 0)),
            pl.BlockSpec((1, d), lambda i, j: (0, 0)),
            pl.BlockSpec((1, 1, d), row),
            pl.BlockSpec((1, 1, d), row_scale),
            pl.BlockSpec((d, tn), lambda i, j: (0, j)),
        ],
        out_specs=pl.BlockSpec((tm, tn), lambda i, j: (i, j)),
        out_shape=jax.ShapeDtypeStruct((geo.m, n), out_dtype),
        scratch_shapes=[pltpu.VMEM((tm, d), BF16)],
        compiler_params=_params(("arbitrary", "arbitrary"), 48),
        name="norm_proj",
    )(xa, g.reshape(1, d), mod, mod, w)


def _qk_prep_kernel(t_ref, g_ref, cos_ref, sin_ref, o_ref):
    t = t_ref[...].astype(F32)
    lane = lax.broadcasted_iota(jnp.int32, t.shape, 1)
    first = lane < DA_QK_DIM
    sq = t * t
    sa = jnp.sum(jnp.where(first, sq, 0.0), axis=-1, keepdims=True)
    sb = jnp.sum(jnp.where(first, 0.0, sq), axis=-1, keepdims=True)
    ms = jnp.where(first, sa, sb) * (1.0 / DA_QK_DIM)
    y = t * lax.rsqrt(ms + EPS) * g_ref[0]
    partner = jnp.where((lane & ROPE_FREQS) == 0,
                        pltpu.roll(y, LANES - ROPE_FREQS, 1), pltpu.roll(y, ROPE_FREQS, 1))
    o_ref[...] = (y * cos_ref[...] + partner * sin_ref[...]).astype(BF16)


def _rope_tables(geo, tm):
    pos = jnp.arange(geo.s)
    inv = ROPE_BASE ** (-jnp.arange(ROPE_FREQS, dtype=F32) * 2.0 / ROPE_AXIS_DIM)
    ang = jnp.stack([pos // GRID_W, pos % GRID_W], axis=-1).astype(F32)[..., None] * inv
    cos, sin = jnp.cos(ang), jnp.sin(ang)
    cos64 = jnp.concatenate([cos, cos], axis=-1).reshape(geo.s, DA_QK_DIM)
    sin64 = jnp.concatenate([-sin, sin], axis=-1).reshape(geo.s, DA_QK_DIM)
    cos_t = jnp.concatenate([jnp.tile(cos64, (1, 2)), jnp.ones((tm, HEAD_DIM), F32)], axis=0)
    sin_t = jnp.concatenate([jnp.tile(sin64, (1, 2)), jnp.zeros((tm, HEAD_DIM), F32)], axis=0)
    return cos_t, sin_t


def _qk_prep(proj, geo, gq, gk):
    tm = _pow2_tile(512, geo.s, geo.n_ctx)
    cos_t, sin_t = _rope_tables(geo, tm)
    scale = DA_QK_DIM ** -0.5 * math.log2(math.e)
    gains = jnp.stack([jnp.tile(gq.astype(F32) * scale, 2), jnp.tile(gk.astype(F32), 2)]).reshape(2, 1, HEAD_DIM)
    n_lat_tiles, per_seq = geo.n_lat // tm, geo.s // tm
    tab = lambda i, j: (jnp.where(i < n_lat_tiles, i % per_seq, per_seq), 0)
    n_blocks = (DA_Q_W * 2) // HEAD_DIM
    return pl.pallas_call(
        _qk_prep_kernel,
        grid=(geo.m // tm, n_blocks),
        in_specs=[
            pl.BlockSpec((tm, HEAD_DIM), lambda i, j: (i, j)),
            pl.BlockSpec((1, 1, HEAD_DIM), lambda i, j: (j // DA_HEADS, 0, 0)),
            pl.BlockSpec((tm, HEAD_DIM), tab),
            pl.BlockSpec((tm, HEAD_DIM), tab),
        ],
        out_specs=pl.BlockSpec((tm, HEAD_DIM), lambda i, j: (i, j)),
        out_shape=jax.ShapeDtypeStruct((geo.m, 2 * DA_Q_W), BF16),
        compiler_params=_params(("arbitrary", "arbitrary"), 32),
        name="qk_prep",
    )(proj, gains, cos_t, sin_t)


def _fori_unrolled(n, unroll, body, init):
    def outer(j, carry):
        for u in range(unroll):
            carry = body(j * unroll + u, carry)
        return carry
    return lax.fori_loop(0, n // unroll, outer, init)


def _fold_lanes(x, op):
    out = x[:, :LANES]
    for j in range(1, x.shape[1] // LANES):
        out = op(out, x[:, j * LANES:(j + 1) * LANES])
    return out


def _attn_kernel(lam_ref, q_ref, kl_ref, kc_ref, vl_ref, vc_ref, g_ref, o_ref, s_lat, s_ctx, *, tk, n_lat_chunks,
                 nq_lat, sub_scale):
    q = q_ref[...]
    tq = q.shape[0]
    lane = lax.broadcasted_iota(jnp.int32, q.shape, 1)
    zero = jnp.zeros_like(q)
    qs = (jnp.where(lane < DA_QK_DIM, q, zero), jnp.where(lane < DA_QK_DIM, zero, q))
    nt = (((1,), (1,)), ((), ()))
    n_chunks = jnp.where(pl.program_id(2) < nq_lat, n_lat_chunks, 0)

    def scores(c, m):
        k = kl_ref[pl.ds(pl.multiple_of(c * tk, tk), tk), :]
        new = []
        for i in range(2):
            s = lax.dot_general(qs[i], k, nt, preferred_element_type=F32)
            s_lat[i, c] = s
            new.append(jnp.maximum(m[i], _fold_lanes(s, jnp.maximum)))
        return tuple(new)

    neg = jnp.full((tq, LANES), -jnp.inf, F32)
    m_lat = _fori_unrolled(n_chunks, math.gcd(n_lat_chunks, 4), scores, (neg, neg))
    m = []
    for i in range(2):
        s = lax.dot_general(qs[i], kc_ref[...], nt, preferred_element_type=F32)
        s_ctx[i] = s
        m.append(jnp.max(jnp.maximum(m_lat[i], _fold_lanes(s, jnp.maximum)), axis=-1, keepdims=True))

    def weighted(c, carry):
        v = vl_ref[pl.ds(pl.multiple_of(c * tk, tk), tk), :]
        new = []
        for i in range(2):
            p = jnp.exp2(s_lat[i, c] - m[i])
            new.append(carry[2 * i] + _fold_lanes(p, jnp.add))
            new.append(carry[2 * i + 1] + jnp.dot(p.astype(BF16), v, preferred_element_type=F32))
        return tuple(new)

    z = jnp.zeros((tq, LANES), F32)
    part = _fori_unrolled(n_chunks, math.gcd(n_lat_chunks, 4), weighted, (z, z, z, z))
    outs = []
    for i in range(2):
        p = jnp.exp2(s_ctx[i] - m[i])
        l = jnp.sum(part[2 * i] + _fold_lanes(p, jnp.add), axis=-1, keepdims=True)
        outs.append((part[2 * i + 1] + jnp.dot(p.astype(BF16), vc_ref[...], preferred_element_type=F32)) / l)
    o = outs[0] - lam_ref[0, 0] * outs[1]
    y = o * lax.rsqrt(jnp.mean(o * o, axis=-1, keepdims=True) + EPS) * g_ref[...]
    o_ref[...] = (y * sub_scale).astype(BF16)


def _diff_attention(qk, proj, geo, lam, subln_g, lam_init):
    tq = _pow2_tile(256, geo.s, geo.c)
    tk = _pow2_tile(512, geo.s)
    nq_lat, nq_ctx = geo.s // tq, geo.c // tq
    k_col0, v_col0 = DA_Q_W // HEAD_DIM, 2 * DA_Q_W // HEAD_DIM
    ctx_blk0 = geo.n_lat // geo.c

    def q_idx(b, h, qi, col0=0):
        lat = b * nq_lat + qi
        ctx = geo.n_lat // tq + b * nq_ctx + (qi - nq_lat)
        return (jnp.where(qi < nq_lat, lat, ctx), col0 + h)

    kernel = functools.partial(_attn_kernel, tk=tk, n_lat_chunks=geo.s // tk, nq_lat=nq_lat,
                               sub_scale=1.0 - lam_init)
    return pl.pallas_call(
        kernel,
        grid=(geo.b, DA_HEADS, nq_lat + nq_ctx),
        in_specs=[
            pl.BlockSpec(memory_space=pltpu.SMEM),
            pl.BlockSpec((tq, HEAD_DIM), q_idx),
            pl.BlockSpec((geo.s, HEAD_DIM), lambda b, h, qi: (b, k_col0 + h)),
            pl.BlockSpec((geo.c, HEAD_DIM), lambda b, h, qi: (ctx_blk0 + b, k_col0 + h)),
            pl.BlockSpec((geo.s, HEAD_DIM), lambda b, h, qi: (b, v_col0 + h)),
            pl.BlockSpec((geo.c, HEAD_DIM), lambda b, h, qi: (ctx_blk0 + b, v_col0 + h)),
            pl.BlockSpec((1, HEAD_DIM), lambda b, h, qi: (0, 0)),
        ],
        out_specs=pl.BlockSpec((tq, HEAD_DIM), q_idx),
        out_shape=jax.ShapeDtypeStruct((geo.m, DA_V_W), BF16),
        scratch_shapes=[pltpu.VMEM((2, geo.s // tk, tq, tk), F32), pltpu.VMEM((2, tq, geo.c), F32)],
        compiler_params=_params(("arbitrary", "arbitrary", "arbitrary"), 40),
        name="diff_attn",
    )(lam.reshape(1, 1), qk, qk, qk, proj, proj, subln_g.reshape(1, HEAD_DIM).astype(F32))


def _gelu(x):
    return 0.5 * x * (1.0 + lax.erf(x * (1.0 / math.sqrt(2.0))))


def _gmlp_kernel(u_ref, v_ref, lng_ref, lnb_ref, ws_ref, bs_ref, o_ref):
    tm = u_ref.shape[0]
    for r in range(tm // GM_CHUNK):
        rows = slice(r * GM_CHUNK, (r + 1) * GM_CHUNK)
        for g in range(GM_GROUPS):
            cols = slice(g * HEAD_DIM, (g + 1) * HEAD_DIM)
            v = _gelu(v_ref[rows, cols].astype(F32))
            mu = jnp.mean(v, axis=-1, keepdims=True)
            var = jnp.mean(jnp.square(v - mu), axis=-1, keepdims=True)
            vn = (v - mu) * lax.rsqrt(var + EPS) * lng_ref[:, cols] + lnb_ref[:, cols]
            s = jnp.dot(ws_ref[g], vn.astype(BF16), preferred_element_type=F32) + bs_ref[:, cols]
            o_ref[rows, cols] = (_gelu(u_ref[rows, cols].astype(F32)) * s).astype(BF16)


def _gmlp(proj, geo, ln_g, ln_b, w_s, b_s):
    tm = _pow2_tile(256, geo.s, geo.c)
    u_blk = (2 * DA_Q_W + DA_V_W) // GM_W
    bs_full = jnp.repeat(b_s.astype(F32).T, HEAD_DIM, axis=1)
    return pl.pallas_call(
        _gmlp_kernel,
        grid=(geo.m // tm,),
        in_specs=[
            pl.BlockSpec((tm, GM_W), lambda i: (i, u_blk)),
            pl.BlockSpec((tm, GM_W), lambda i: (i, u_blk + 1)),
            pl.BlockSpec((1, GM_W), lambda i: (0, 0)),
            pl.BlockSpec((1, GM_W), lambda i: (0, 0)),
            pl.BlockSpec((GM_GROUPS, GM_CHUNK, GM_CHUNK), lambda i: (0, 0, 0)),
            pl.BlockSpec((GM_CHUNK, GM_W), lambda i: (0, 0)),
        ],
        out_specs=pl.BlockSpec((tm, GM_W), lambda i: (i, 0)),
        out_shape=jax.ShapeDtypeStruct((geo.m, GM_W), BF16),
        compiler_params=_params(("arbitrary",), 32),
        name="gmlp",
    )(proj, proj, ln_g.reshape(1, GM_W).astype(F32), ln_b.reshape(1, GM_W).astype(F32), w_s.astype(BF16), bs_full)


def _pack_bf16_pair(lo, hi):
    lo_bits = lax.bitcast_convert_type(lo.astype(BF16).astype(F32), U32)
    hi_bits = lax.bitcast_convert_type(hi.astype(BF16).astype(F32), U32)
    return (lo_bits >> 16) | (hi_bits & jnp.uint32(0xFFFF0000))


def _resid_norm_route(y, x_ref, gate_ref, g_ref, sh_ref, sc_ref, rw_ref, rb_ref, xn_ref, hp_ref, lg_ref):
    xn = x_ref[...] + gate_ref[0] * y
    xn_ref[...] = xn
    h = _modulated_norm(xn, g_ref[...], sh_ref[0], sc_ref[0])
    lg_ref[...] = jnp.dot(h, rw_ref[...], preferred_element_type=F32, precision=HIGHEST) + rb_ref[...]
    half = h.shape[1] // 2
    hp_ref[...] = _pack_bf16_pair(h[:, :half], h[:, half:])


def _epilogue_specs(geo, tm, d, grid_rank):
    def ix(f):
        return (lambda i: f(i)) if grid_rank == 1 else (lambda i, k: f(i))
    mod = lambda k: ix(lambda i: (geo.mod_row(i, tm), 0, k))
    in_specs = [
        pl.BlockSpec((tm, d), ix(lambda i: (i, 0))),
        pl.BlockSpec((1, 1, d), mod(2)),
        pl.BlockSpec((1, d), ix(lambda i: (0, 0))),
        pl.BlockSpec((1, 1, d), mod(3)),
        pl.BlockSpec((1, 1, d), mod(4)),
        pl.BlockSpec((d, ROUTER_PAD), ix(lambda i: (0, 0))),
        pl.BlockSpec((1, ROUTER_PAD), ix(lambda i: (0, 0))),
    ]
    out_specs = [
        pl.BlockSpec((tm, d), ix(lambda i: (i, 0))),
        pl.BlockSpec((tm, d // 2), ix(lambda i: (i, 0))),
        pl.BlockSpec((tm, ROUTER_PAD), ix(lambda i: (i, 0))),
    ]
    return in_specs, out_specs


def _epilogue_out_shapes(n_rows, d):
    return [jax.ShapeDtypeStruct((n_rows, d), F32), jax.ShapeDtypeStruct((n_rows, d // 2), U32),
            jax.ShapeDtypeStruct((n_rows, ROUTER_PAD), F32)]


def _router_operands(router_w, router_b):
    d = router_w.shape[0]
    rw = jnp.zeros((d, ROUTER_PAD), F32).at[:, :N_EXPERTS].set(router_w.astype(F32))
    rb = jnp.zeros((1, ROUTER_PAD), F32).at[0, :N_EXPERTS].set(router_b.astype(F32))
    return rw, rb


def _attn_out_kernel(a_ref, gm_ref, wo_ref, x_ref, gate_ref, g_ref, sh_ref, sc_ref, rw_ref, rb_ref,
                     xn_ref, hp_ref, lg_ref):
    y = jnp.dot(a_ref[...], wo_ref[:DA_V_W, :], preferred_element_type=F32)
    y = y + jnp.dot(gm_ref[...], wo_ref[DA_V_W:, :], preferred_element_type=F32)
    _resid_norm_route(y, x_ref, gate_ref, g_ref, sh_ref, sc_ref, rw_ref, rb_ref, xn_ref, hp_ref, lg_ref)


def _attn_out(attn, gm, w_o, xa, geo, mod, g2, rw, rb):
    d = xa.shape[1]
    tm = _pow2_tile(512, geo.s, geo.n_ctx)
    ep_in, ep_out = _epilogue_specs(geo, tm, d, 1)
    return pl.pallas_call(
        _attn_out_kernel,
        grid=(geo.m // tm,),
        in_specs=[
            pl.BlockSpec((tm, DA_V_W), lambda i: (i, 0)),
            pl.BlockSpec((tm, GM_W), lambda i: (i, 0)),
            pl.BlockSpec((DA_V_W + GM_W, d), lambda i: (0, 0)),
        ] + ep_in,
        out_specs=ep_out,
        out_shape=_epilogue_out_shapes(geo.m, d),
        compiler_params=_params(("arbitrary",), 52),
        name="attn_out",
    )(attn, gm, w_o, xa, mod, g2.reshape(1, d), mod, mod, rw, rb)


def _route(logits, n_tok):
    top_val, top_idx = lax.top_k(logits[:, :N_EXPERTS], TOP_K)
    gates = jax.nn.softmax(top_val, axis=-1)
    n_asg = n_tok * TOP_K
    n_tiles = n_asg // MOE_TM + N_EXPERTS
    flat_e = top_idx.reshape(-1).astype(jnp.int32)
    order = jnp.argsort(flat_e, stable=True).astype(jnp.int32)
    experts = jnp.arange(N_EXPERTS, dtype=jnp.int32)
    counts = jnp.sum((flat_e[:, None] == experts[None, :]).astype(jnp.int32), axis=0)
    grp_start = jnp.cumsum(counts) - counts
    tiles_e = (counts + MOE_TM - 1) // MOE_TM
    tile_end = jnp.cumsum(tiles_e)
    tile_start = tile_end - tiles_e
    n_valid = tile_end[-1]
    t = jnp.arange(n_tiles, dtype=jnp.int32)
    t_eff = jnp.minimum(t, n_valid - 1)
    tile_e = jnp.minimum(jnp.searchsorted(tile_end, t_eff, side="right"), N_EXPERTS - 1).astype(jnp.int32)
    tile_off = (t_eff - tile_start[tile_e]) * MOE_TM
    tile_rows = jnp.where(t < n_valid, jnp.clip(counts[tile_e] - tile_off, 0, MOE_TM), 0).astype(jnp.int32)
    r = jnp.arange(MOE_TM, dtype=jnp.int32)[None, :]
    src = jnp.clip((grp_start[tile_e] + tile_off)[:, None] + r, 0, n_asg - 1)
    asg = jnp.where(r < tile_rows[:, None], order[src], 0)
    row_tok = (asg // TOP_K).reshape(n_tiles, 1, MOE_TM)
    row_slot = ((asg % TOP_K) * n_tok + asg // TOP_K).reshape(n_tiles, 1, MOE_TM)
    return tile_e, tile_rows, row_tok, row_slot, gates


def _unpack_bf16_pair(u):
    return lax.bitcast_convert_type(u << 16, F32), lax.bitcast_convert_type(u & jnp.uint32(0xFFFF0000), F32)


def _row_blocks(n_rows):
    return (n_rows + MOE_RB - 1) // MOE_RB


def _moe_kernel(te_ref, tr_ref, tok_ref, tok_next_ref, slot_ref, hp_hbm, w1g_ref, w1l_ref, b1g_ref, b1l_ref,
                w2_ref, b2_ref, ys_hbm, xbuf, xs, act, w2b, ystage, gsem, ssem):
    t, s = pl.program_id(0), pl.program_id(1)
    rows = tr_ref[t]
    nrb = _row_blocks(rows)
    nf = act.shape[0]
    half = xbuf.shape[1]

    def gather_rows(rows_ref, n_rows):
        def body(j, _):
            for u in range(MOE_ISSUE_UNROLL):
                r = j * MOE_ISSUE_UNROLL + u
                pltpu.make_async_copy(hp_hbm.at[pl.ds(rows_ref[0, 0, r], 1), :], xbuf.at[pl.ds(r, 1), :], gsem.at[0]).start()
            return 0
        lax.fori_loop(0, n_rows // MOE_ISSUE_UNROLL, body, 0)

    @pl.when(s == 0)
    def _():
        @pl.when(t == 0)
        def _():
            gather_rows(tok_ref, nrb * MOE_RB)

        def wait_block(i, _):
            r0 = pl.multiple_of(i * MOE_RB, MOE_RB)
            pltpu.make_async_copy(hp_hbm.at[pl.ds(0, MOE_RB), :], xbuf.at[pl.ds(r0, MOE_RB), :], gsem.at[0]).wait()
            return 0
        lax.fori_loop(0, nrb, wait_block, 0)

        def unpack_block(i, _):
            r0 = pl.multiple_of(i * MOE_RB, MOE_RB)
            lo, hi = _unpack_bf16_pair(xbuf[pl.ds(r0, MOE_RB), :])
            xs[pl.ds(r0, MOE_RB), :half] = lo.astype(BF16)
            xs[pl.ds(r0, MOE_RB), half:] = hi.astype(BF16)
            return 0
        lax.fori_loop(0, nrb, unpack_block, 0)

        @pl.when(t + 1 < pl.num_programs(0))
        def _():
            gather_rows(tok_next_ref, _row_blocks(tr_ref[t + 1]) * MOE_RB)

    @pl.when(jnp.logical_and(s < nf, nrb > 0))
    def _():
        w1g = w1g_ref[0, 0].astype(BF16)
        w1l = w1l_ref[0, 0].astype(BF16)
        w2b[pl.ds(pl.multiple_of(s * MOE_TF, MOE_TF), MOE_TF), :] = w2_ref[0, 0].astype(BF16)

        def block(i, _):
            r0 = pl.multiple_of(i * MOE_RB, MOE_RB)
            x = xs[pl.ds(r0, MOE_RB), :]
            g = jnp.minimum(jnp.dot(x, w1g, preferred_element_type=F32) + b1g_ref[0, 0], SWIGLU_LIMIT)
            lin = jnp.clip(jnp.dot(x, w1l, preferred_element_type=F32) + b1l_ref[0, 0], -SWIGLU_LIMIT, SWIGLU_LIMIT)
            act[s, pl.ds(r0, MOE_RB), :] = (g * jax.nn.sigmoid(SWIGLU_ALPHA * g) * (lin + 1.0)).astype(BF16)
            return 0
        lax.fori_loop(0, nrb, block, 0)

    def scatter_wait(stage_slot, n_rows):
        pltpu.make_async_copy(ystage.at[stage_slot, pl.ds(0, n_rows), :], ys_hbm.at[pl.ds(0, n_rows), :],
                              ssem.at[stage_slot]).wait()

    @pl.when(jnp.logical_and(s == nf, nrb > 0))
    def _():
        def block(i, _):
            r0 = pl.multiple_of(i * MOE_RB, MOE_RB)
            stage_slot = i % 2
            a = jnp.concatenate([act[f, pl.ds(r0, MOE_RB), :] for f in range(nf)], axis=1)
            y = jnp.dot(a, w2b[...], preferred_element_type=F32) + b2_ref[0, 0]

            @pl.when(i >= 2)
            def _():
                scatter_wait(stage_slot, MOE_RB)

            ystage[stage_slot] = _pack_bf16_pair(y[:, :half], y[:, half:])
            n_valid = jnp.minimum(rows - r0, MOE_RB)

            def issue(r):
                pltpu.make_async_copy(ystage.at[stage_slot, pl.ds(r, 1), :],
                                      ys_hbm.at[pl.ds(slot_ref[0, 0, r0 + r], 1), :], ssem.at[stage_slot]).start()

            def issue_many(j, _):
                for u in range(MOE_ISSUE_UNROLL):
                    issue(j * MOE_ISSUE_UNROLL + u)
                return 0
            n_main = n_valid // MOE_ISSUE_UNROLL
            lax.fori_loop(0, n_main, issue_many, 0)

            def issue_one(r, _):
                issue(r)
                return 0
            lax.fori_loop(n_main * MOE_ISSUE_UNROLL, n_valid, issue_one, 0)
            return 0
        lax.fori_loop(0, nrb, block, 0)

        last = nrb - 1
        def wait_one(r, _):
            scatter_wait(last % 2, 1)
            return 0
        lax.fori_loop(0, rows - last * MOE_RB, wait_one, 0)

        @pl.when(nrb >= 2)
        def _():
            scatter_wait((last - 1) % 2, MOE_RB)


def _moe_experts(hp, n_tok, tile_e, tile_rows, row_tok, row_slot, w1, b1, w2, b2, li):
    n_tiles = row_tok.shape[0]
    d = w2.shape[-1]
    nf = D_EXPERT // MOE_TF
    fz = lambda s, tr, t: jnp.where(tr[t] > 0, jnp.minimum(s, nf - 1), nf - 1)
    b1r = b1.reshape(b1.shape[0], N_EXPERTS, 1, 2 * D_EXPERT)
    b2r = b2.reshape(b2.shape[0], N_EXPERTS, 1, d)
    rows_spec = lambda f: pl.BlockSpec((1, 1, MOE_TM), f, memory_space=pltpu.SMEM)
    grid_spec = pltpu.PrefetchScalarGridSpec(
        num_scalar_prefetch=2,
        grid=(n_tiles, nf + 1),
        in_specs=[
            rows_spec(lambda t, s, te, tr: (t, 0, 0)),
            rows_spec(lambda t, s, te, tr: (jnp.minimum(t + 1, n_tiles - 1), 0, 0)),
            rows_spec(lambda t, s, te, tr: (t, 0, 0)),
            pl.BlockSpec(memory_space=pl.ANY),
            pl.BlockSpec((1, 1, d, MOE_TF), lambda t, s, te, tr: (li, te[t], 0, fz(s, tr, t))),
            pl.BlockSpec((1, 1, d, MOE_TF), lambda t, s, te, tr: (li, te[t], 0, nf + fz(s, tr, t))),
            pl.BlockSpec((1, 1, 1, MOE_TF), lambda t, s, te, tr: (li, te[t], 0, fz(s, tr, t))),
            pl.BlockSpec((1, 1, 1, MOE_TF), lambda t, s, te, tr: (li, te[t], 0, nf + fz(s, tr, t))),
            pl.BlockSpec((1, 1, MOE_TF, d), lambda t, s, te, tr: (li, te[t], fz(s, tr, t), 0)),
            pl.BlockSpec((1, 1, 1, d), lambda t, s, te, tr: (li, te[t], 0, 0)),
        ],
        out_specs=pl.BlockSpec(memory_space=pl.ANY),
        scratch_shapes=[
            pltpu.VMEM((MOE_TM, d // 2), U32),
            pltpu.VMEM((MOE_TM, d), BF16),
            pltpu.VMEM((nf, MOE_TM, MOE_TF), BF16),
            pltpu.VMEM((D_EXPERT, d), BF16),
            pltpu.VMEM((2, MOE_RB, d // 2), U32),
            pltpu.SemaphoreType.DMA((1,)),
            pltpu.SemaphoreType.DMA((2,)),
        ],
    )
    return pl.pallas_call(
        _moe_kernel,
        grid_spec=grid_spec,
        out_shape=jax.ShapeDtypeStruct((TOP_K * n_tok, d // 2), U32),
        compiler_params=_params(("arbitrary", "arbitrary"), 60),
        name="moe_experts",
    )(tile_e, tile_rows, row_tok, row_tok, row_slot, hp, w1, w1, b1r, b1r, w2, b2r)


def _combine_kernel(ys_ref, gate_ref, x_ref, g5_ref, o_ref):
    half = ys_ref.shape[2]
    gate = gate_ref[...]
    acc_lo = acc_hi = None
    for k in range(TOP_K):
        lo, hi = _unpack_bf16_pair(ys_ref[k])
        g = gate[:, k:k + 1]
        acc_lo = g * lo if k == 0 else acc_lo + g * lo
        acc_hi = g * hi if k == 0 else acc_hi + g * hi
    o_ref[:, :half] = x_ref[:, :half] + g5_ref[0, :, :half] * acc_lo
    o_ref[:, half:] = x_ref[:, half:] + g5_ref[0, :, half:] * acc_hi


def _moe_combine(ys, gates, xn, geo, n_tok, mod):
    d = xn.shape[1]
    tc = _pow2_tile(COMBINE_TC, geo.s, geo.n_ctx)
    return pl.pallas_call(
        _combine_kernel,
        grid=(n_tok // tc,),
        in_specs=[
            pl.BlockSpec((TOP_K, tc, d // 2), lambda i: (0, i, 0)),
            pl.BlockSpec((tc, TOP_K), lambda i: (i, 0)),
            pl.BlockSpec((tc, d), lambda i: (i, 0)),
            pl.BlockSpec((1, 1, d), lambda i: (geo.mod_row(i, tc), 0, 5)),
        ],
        out_specs=pl.BlockSpec((tc, d), lambda i: (i, 0)),
        out_shape=jax.ShapeDtypeStruct((n_tok, d), F32),
        compiler_params=_params(("arbitrary",), 40),
        name="moe_combine",
    )(ys.reshape(TOP_K, n_tok, d // 2), gates, xn, mod)


def _moe_layer(xn, hp, logits, geo, n_tok, mod, w1, b1, w2, b2, li):
    tile_e, tile_rows, row_tok, row_slot, gates = _route(logits[:n_tok], n_tok)
    ys = _moe_experts(hp, n_tok, tile_e, tile_rows, row_tok, row_slot, w1, b1, w2, b2, li)
    return _moe_combine(ys, gates, xn, geo, n_tok, mod)


def _ssd_prep_kernel(raw_ref, bias_ref, aneg_ref, dt_ref, ac_ref):
    tm = raw_ref.shape[0]
    li = lax.broadcasted_iota(jnp.int32, (SSD_CHUNK, SSD_CHUNK), 0)
    si = lax.broadcasted_iota(jnp.int32, (SSD_CHUNK, SSD_CHUNK), 1)
    tri_f = (si <= li).astype(F32)
    tri_b = (si >= li).astype(F32)
    lane = lax.broadcasted_iota(jnp.int32, (SSD_CHUNK, SSD_DT_W), 1)
    for c in range(tm // SSD_CHUNK):
        rows = slice(c * SSD_CHUNK, (c + 1) * SSD_CHUNK)
        v = raw_ref[rows, :] + bias_ref[...]
        dt = jnp.maximum(v, 0.0) + jnp.log1p(jnp.exp(-jnp.abs(v)))
        a = dt * aneg_ref[...]
        fwd = jnp.dot(tri_f, a, preferred_element_type=F32, precision=HIGHEST)
        bwd = jnp.dot(tri_b, a, preferred_element_type=F32, precision=HIGHEST)
        dt_ref[rows, :] = dt
        ac_ref[rows, :] = jnp.where(lane < SSD_HEADS, fwd, bwd)


def _ssd_prep(dt_raw, geo, dt_bias, a_log):
    tm = _pow2_tile(512, geo.s, geo.c)
    a_neg = -jnp.exp(a_log.astype(F32)).reshape(1, SSD_DT_W)
    spec = pl.BlockSpec((tm, SSD_DT_W), lambda i: (i, 0))
    vec = pl.BlockSpec((1, SSD_DT_W), lambda i: (0, 0))
    return pl.pallas_call(
        _ssd_prep_kernel,
        grid=(geo.m // tm,),
        in_specs=[spec, vec, vec],
        out_specs=[spec, spec],
        out_shape=[jax.ShapeDtypeStruct((geo.m, SSD_DT_W), F32)] * 2,
        compiler_params=_params(("arbitrary",), 32),
        name="ssd_prep",
    )(dt_raw, dt_bias.astype(F32).reshape(1, SSD_DT_W), a_neg)


CONV_HALO = 16


def _conv_kernel(prev_ref, cur_ref, next_ref, w_ref, b_ref, o_ref, ext, *, geo, tm):
    i = pl.program_id(0)
    n_lat_tiles = geo.n_lat // tm
    per_lat, per_ctx = geo.s // tm, geo.c // tm
    lat = i < n_lat_tiles
    p = jnp.where(lat, i % per_lat, (i - n_lat_tiles) % per_ctx)
    first = p == 0
    last = jnp.where(lat, p == per_lat - 1, p == per_ctx - 1)
    ext[0:CONV_HALO, :] = jnp.where(first, 0.0, prev_ref[...].astype(F32))
    ext[CONV_HALO:CONV_HALO + tm, :] = cur_ref[...].astype(F32)
    ext[CONV_HALO + tm:, :] = jnp.where(last, 0.0, next_ref[...].astype(F32))
    pad = SSD_CONV_W // 2
    acc = w_ref[0:1, :] * ext[pl.ds(CONV_HALO - pad, tm), :]
    for k in range(1, SSD_CONV_W):
        acc = acc + w_ref[k:k + 1, :] * ext[pl.ds(CONV_HALO - pad + k, tm), :]
    acc = acc + b_ref[...]
    o_ref[...] = (acc * jax.nn.sigmoid(acc)).astype(BF16)


def _conv_silu(zx, geo, conv_w, conv_b):
    tm = _pow2_tile(256, geo.s, geo.c)
    tc = 1024
    col0 = SSD_D_INNER // tc
    hb = tm // CONV_HALO
    n_halo_blocks = geo.m // CONV_HALO
    return pl.pallas_call(
        functools.partial(_conv_kernel, geo=geo, tm=tm),
        grid=(geo.m // tm, SSD_CONV_CH // tc),
        in_specs=[
            pl.BlockSpec((CONV_HALO, tc), lambda i, j: (jnp.maximum(i * hb - 1, 0), col0 + j)),
            pl.BlockSpec((tm, tc), lambda i, j: (i, col0 + j)),
            pl.BlockSpec((CONV_HALO, tc), lambda i, j: (jnp.minimum((i + 1) * hb, n_halo_blocks - 1), col0 + j)),
            pl.BlockSpec((SSD_CONV_W, tc), lambda i, j: (0, j)),
            pl.BlockSpec((1, tc), lambda i, j: (0, j)),
        ],
        out_specs=pl.BlockSpec((tm, tc), lambda i, j: (i, j)),
        out_shape=jax.ShapeDtypeStruct((geo.m, SSD_CONV_CH), BF16),
        scratch_shapes=[pltpu.VMEM((tm + 2 * CONV_HALO, tc), F32)],
        compiler_params=_params(("arbitrary", "arbitrary"), 32),
        name="ssd_conv",
    )(zx, zx, zx, conv_w.astype(F32), conv_b.astype(F32).reshape(1, SSD_CONV_CH))


def _ssd_scan_kernel(xs_ref, b_ref, c_ref, dt_ref, ac_ref, y_ref, state, xg, bt_g, c_g, dt_g, ac_g, ac_t, y_g):
    d, s = pl.program_id(1), pl.program_id(2)
    q = SSD_CHUNK
    is_fwd = d == 0

    @pl.when(s == 0)
    def _():
        state[...] = jnp.zeros(state.shape, F32)

    dt2, ac2 = dt_ref[...], ac_ref[...]
    dtc = jnp.where(is_fwd, dt2[:, :SSD_HEADS], dt2[:, SSD_HEADS:])
    acc = jnp.where(is_fwd, ac2[:, :SSD_HEADS], ac2[:, SSD_HEADS:])
    ac_t[...] = ac2.T
    b_f32 = b_ref[...].astype(F32)
    for g in range(SSD_GROUPS):
        xg[g] = xs_ref[:, g * SSD_GROUP_W:(g + 1) * SSD_GROUP_W]
        bt_g[g] = b_f32[:, g * SSD_STATE:(g + 1) * SSD_STATE].T.astype(BF16)
        c_g[g] = c_ref[:, g * SSD_STATE:(g + 1) * SSD_STATE]
        dt_g[g] = dtc[:, g * SSD_HPG:(g + 1) * SSD_HPG]
        ac_g[g] = acc[:, g * SSD_HPG:(g + 1) * SSD_HPG]

    li = lax.broadcasted_iota(jnp.int32, (q, q), 0)
    si = lax.broadcasted_iota(jnp.int32, (q, q), 1)
    mask = jnp.where(is_fwd, li - si, si - li) >= 0
    lane = lax.broadcasted_iota(jnp.int32, (q, LANES), 1)
    lo_half = lane < SSD_HEAD_DIM
    lo_row = lo_half[0:1, :]

    def group(g, _):
        cg = c_g[g]
        cb = jnp.dot(cg, bt_g[g], preferred_element_type=F32)
        dtg, acg = dt_g[g], ac_g[g]
        xf = xg[g].astype(F32)
        a_cols, d_cols, decays = [], [], []
        for j in range(SSD_HPG):
            a_col = jnp.broadcast_to(acg[:, j:j + 1], (q, q))
            a_row = ac_t[pl.ds(d * SSD_HEADS + g * SSD_HPG + j, 1), :]
            seg = a_col - a_row
            decays.append((cb * jnp.exp(jnp.where(mask, seg, -jnp.inf))).astype(BF16))
            a_cols.append(a_col)
            d_cols.append(jnp.broadcast_to(dtg[:, j:j + 1], (q, q)))
        for p in range(SSD_HPG // 2):
            cols = slice(p * LANES, (p + 1) * LANES)
            acx = jnp.where(lo_half, a_cols[2 * p], a_cols[2 * p + 1])
            dtx = jnp.where(lo_half, d_cols[2 * p], d_cols[2 * p + 1])
            a_last = jnp.where(is_fwd, acx[q - 1:q, :], acx[0:1, :])
            xdt = xf[:, cols] * dtx
            xdt_b = xdt.astype(BF16)
            zero = jnp.zeros_like(xdt_b)
            rhs = jnp.concatenate([jnp.where(lo_half, xdt_b, zero), jnp.where(lo_half, zero, xdt_b)], axis=0)
            lhs = jnp.concatenate([decays[2 * p], decays[2 * p + 1]], axis=1)
            st = state[g, :, cols]
            y = jnp.dot(lhs, rhs, preferred_element_type=F32)
            y = y + jnp.dot(cg, st.astype(BF16), preferred_element_type=F32) * jnp.exp(acx)
            y_g[g, :, cols] = y
            wx = (xdt * jnp.exp(a_last - acx)).astype(BF16)
            state[g, :, cols] = st * jnp.exp(a_last) + jnp.dot(bt_g[g], wx, preferred_element_type=F32)
        return 0

    lax.fori_loop(0, SSD_GROUPS, group, 0)
    for g in range(SSD_GROUPS):
        y_ref[0, :, g * SSD_GROUP_W:(g + 1) * SSD_GROUP_W] = y_g[g].astype(BF16)


def _ssd_scan(cv, dt, ac, geo):
    q = SSD_CHUNK
    ncc, nlc = geo.c // q, geo.s // q
    bc_col0 = SSD_D_INNER // (SSD_GROUPS * SSD_STATE)

    def lat_chunk(d, s):
        c = jnp.maximum(s - ncc, 0)
        return jnp.where(d == 0, c, nlc - 1 - c)

    def row_blk(b, d, s):
        ctx_c = jnp.where(d == 0, s, ncc - 1 - s)
        return jnp.where(s < ncc, geo.b * nlc + b * ncc + ctx_c, b * nlc + lat_chunk(d, s))

    return pl.pallas_call(
        _ssd_scan_kernel,
        grid=(geo.b, 2, ncc + nlc),
        in_specs=[
            pl.BlockSpec((q, SSD_D_INNER), lambda b, d, s: (row_blk(b, d, s), 0)),
            pl.BlockSpec((q, SSD_GROUPS * SSD_STATE), lambda b, d, s: (row_blk(b, d, s), bc_col0 + 2 * d)),
            pl.BlockSpec((q, SSD_GROUPS * SSD_STATE), lambda b, d, s: (row_blk(b, d, s), bc_col0 + 2 * d + 1)),
            pl.BlockSpec((q, SSD_DT_W), lambda b, d, s: (row_blk(b, d, s), 0)),
            pl.BlockSpec((q, SSD_DT_W), lambda b, d, s: (row_blk(b, d, s), 0)),
        ],
        out_specs=pl.BlockSpec((1, q, SSD_D_INNER), lambda b, d, s: (d, b * nlc + lat_chunk(d, s), 0)),
        out_shape=jax.ShapeDtypeStruct((2, geo.n_lat, SSD_D_INNER), BF16),
        scratch_shapes=[
            pltpu.VMEM((SSD_GROUPS, SSD_STATE, SSD_GROUP_W), F32),
            pltpu.VMEM((SSD_GROUPS, q, SSD_GROUP_W), BF16),
            pltpu.VMEM((SSD_GROUPS, SSD_STATE, q), BF16),
            pltpu.VMEM((SSD_GROUPS, q, SSD_STATE), BF16),
            pltpu.VMEM((SSD_GROUPS, q, SSD_HPG), F32),
            pltpu.VMEM((SSD_GROUPS, q, SSD_HPG), F32),
            pltpu.VMEM((SSD_DT_W, q), F32),
            pltpu.VMEM((SSD_GROUPS, q, SSD_GROUP_W), F32),
        ],
        compiler_params=_params(("arbitrary", "arbitrary", "arbitrary"), 40),
        name="ssd_scan",
    )(cv, cv, cv, dt, ac)


def _ssd_out_kernel(yf_ref, yb_ref, xs_ref, z_ref, skip_ref, ng_ref, wo_ref, x_ref, gate_ref, g_ref, sh_ref,
                    sc_ref, rw_ref, rb_ref, xn_ref, hp_ref, lg_ref, acc):
    k = pl.program_id(1)
    y = yf_ref[0].astype(F32) + yb_ref[0].astype(F32) + skip_ref[...] * xs_ref[...].astype(F32)
    z = z_ref[...].astype(F32)
    t = y * (z * jax.nn.sigmoid(z))
    parts = []
    for gg in range(t.shape[1] // SSD_GROUP_W):
        tg = t[:, gg * SSD_GROUP_W:(gg + 1) * SSD_GROUP_W]
        parts.append(tg * lax.rsqrt(jnp.mean(tg * tg, axis=-1, keepdims=True) + EPS))
    tn = (jnp.concatenate(parts, axis=1) * ng_ref[...]).astype(BF16)
    contrib = jnp.dot(tn, wo_ref[...], preferred_element_type=F32)

    @pl.when(k == 0)
    def _():
        acc[...] = contrib

    @pl.when(k > 0)
    def _():
        acc[...] += contrib

    @pl.when(k == pl.num_programs(1) - 1)
    def _():
        _resid_norm_route(acc[...], x_ref, gate_ref, g_ref, sh_ref, sc_ref, rw_ref, rb_ref, xn_ref, hp_ref, lg_ref)


def _ssd_out(yd, cv, zx, skip_x, norm_g, w_o, xa, geo, mod, g2, rw, rb):
    d = xa.shape[1]
    tm = _pow2_tile(512, geo.s)
    tk = 1024
    ep_in, ep_out = _epilogue_specs(geo, tm, d, 2)
    return pl.pallas_call(
        _ssd_out_kernel,
        grid=(geo.n_lat // tm, SSD_D_INNER // tk),
        in_specs=[
            pl.BlockSpec((1, tm, tk), lambda i, k: (0, i, k)),
            pl.BlockSpec((1, tm, tk), lambda i, k: (1, i, k)),
            pl.BlockSpec((tm, tk), lambda i, k: (i, k)),
            pl.BlockSpec((tm, tk), lambda i, k: (i, k)),
            pl.BlockSpec((1, tk), lambda i, k: (0, k)),
            pl.BlockSpec((1, tk), lambda i, k: (0, k)),
            pl.BlockSpec((tk, d), lambda i, k: (k, 0)),
        ] + ep_in,
        out_specs=ep_out,
        out_shape=_epilogue_out_shapes(geo.n_lat, d),
        scratch_shapes=[pltpu.VMEM((tm, d), F32)],
        compiler_params=_params(("arbitrary", "arbitrary"), 52),
        name="ssd_out",
    )(yd, yd, cv, zx, skip_x, norm_g.astype(F32).reshape(1, SSD_D_INNER), w_o, xa, mod, g2.reshape(1, d), mod, mod,
      rw, rb)


def kernel(x, c, ctx, c_ctx, ada_w, ada_b, norm_g, attn_w_in, attn_w_o, da_q_norm_g, da_k_norm_g, da_lambda,
           da_subln_g, gm_ln_g, gm_ln_b, gm_w_s, gm_b_s, ssd_w_in, ssd_conv_w, ssd_conv_b, ssd_dt_bias, ssd_a_log,
           ssd_d_skip, ssd_norm_g, ssd_w_o, router_w, router_b, moe_w1, moe_b1, moe_w2, moe_b2):
    bsz, n, d = x.shape
    geo = _Geo(bsz, n, ctx.shape[1])
    assert ada_w.shape[0] == 2 and d == D_MODEL, "even (attention/gMLP) layer followed by a final odd (SSD) layer"

    n_rows = -(-(bsz + 1) // 8) * 8
    cond = jnp.zeros((n_rows, d), F32).at[:bsz].set(c).at[bsz].set(c_ctx)
    mods = _adaln_mods(cond, ada_w, ada_b)
    xa = jnp.concatenate([x.reshape(geo.n_lat, d), ctx.reshape(geo.n_ctx, d)], axis=0)

    mod = mods[0].reshape(n_rows, 1, 6 * d)
    proj = _norm_proj(xa, geo, norm_g[0, 0], mod, attn_w_in[0].astype(BF16), BF16, 1024)
    qk = _qk_prep(proj, geo, da_q_norm_g[0], da_k_norm_g[0])
    lam_init = 0.8 - 0.6 * math.exp(-0.3 * 0)
    lv = da_lambda[0].astype(F32)
    lam = jnp.exp(jnp.sum(lv[0] * lv[1])) - jnp.exp(jnp.sum(lv[2] * lv[3])) + lam_init
    attn = _diff_attention(qk, proj, geo, lam, da_subln_g[0], lam_init)
    gm = _gmlp(proj, geo, gm_ln_g[0], gm_ln_b[0], gm_w_s[0], gm_b_s[0])
    rw, rb = _router_operands(router_w[0], router_b[0])
    xn, hp, logits = _attn_out(attn, gm, attn_w_o[0].astype(BF16), xa, geo, mod, norm_g[0, 1], rw, rb)
    xa = _moe_layer(xn, hp, logits, geo, geo.m, mod, moe_w1, moe_b1, moe_w2, moe_b2, 0)

    mod = mods[1].reshape(n_rows, 1, 6 * d)
    w_in = ssd_w_in[0].astype(BF16)
    n_zx = SSD_D_INNER + SSD_CONV_CH
    zx = _norm_proj(xa, geo, norm_g[1, 0], mod, w_in[:, :n_zx], BF16, 1024)
    dt_raw = _norm_proj(xa, geo, norm_g[1, 0], mod, w_in[:, n_zx:], F32, SSD_DT_W)
    dt, ac = _ssd_prep(dt_raw, geo, ssd_dt_bias[0], ssd_a_log[0])
    cv = _conv_silu(zx, geo, ssd_conv_w[0], ssd_conv_b[0])
    yd = _ssd_scan(cv, dt, ac, geo)
    skip_x = jnp.repeat(ssd_d_skip[0].astype(F32).reshape(-1), SSD_HEAD_DIM).reshape(1, SSD_D_INNER)
    rw, rb = _router_operands(router_w[1], router_b[1])
    xn, hp, logits = _ssd_out(yd, cv, zx, skip_x, ssd_norm_g[0], ssd_w_o[0].astype(BF16), xa, geo, mod,
                              norm_g[1, 1], rw, rb)
    out = _moe_layer(xn, hp, logits, geo, geo.n_lat, mod, moe_w1, moe_b1, moe_w2, moe_b2, 1)
    return out.reshape(bsz, n, d)
```

```python
import functools
import math

import jax
import jax.numpy as jnp
from jax import lax
from jax.experimental import pallas as pl
from jax.experimental.pallas import tpu as pltpu

F32 = jnp.float32
BF16 = jnp.bfloat16
U32 = jnp.uint32
HIGHEST = lax.Precision.HIGHEST

D_MODEL = 2048
GRID_W = 64
EPS = 1e-6
LANES = 128
HEAD_DIM = 128
DA_HEADS = 8
DA_QK_DIM = 64
ROPE_BASE = 10000.0
ROPE_AXIS_DIM = DA_QK_DIM // 2
ROPE_FREQS = ROPE_AXIS_DIM // 2
GM_GROUPS = 8
GM_CHUNK = 128
DA_Q_W = DA_HEADS * 2 * DA_QK_DIM
DA_V_W = DA_HEADS * HEAD_DIM
GM_W = GM_GROUPS * HEAD_DIM
EVEN_IN = 2 * DA_Q_W + DA_V_W + 2 * GM_W
SSD_D_INNER = 2 * D_MODEL
SSD_HEAD_DIM = 64
SSD_HEADS = SSD_D_INNER // SSD_HEAD_DIM
SSD_GROUPS = 8
SSD_HPG = SSD_HEADS // SSD_GROUPS
SSD_GROUP_W = SSD_HPG * SSD_HEAD_DIM
SSD_STATE = 128
SSD_CONV_W = 7
SSD_CHUNK = 128
SSD_BC_W = 2 * 2 * SSD_GROUPS * SSD_STATE
SSD_CONV_CH = SSD_D_INNER + SSD_BC_W
SSD_DT_W = 2 * SSD_HEADS
N_EXPERTS = 32
TOP_K = 4
D_EXPERT = D_MODEL
SWIGLU_LIMIT = 7.0
SWIGLU_ALPHA = 1.702

MOE_TM = 2304
MOE_RB = 256
MOE_TF = 256
MOE_P1_BLOCKS = 3
SUBLANES = 8
COMBINE_TC = 512
ROUTER_PAD = 128
MIB = 2 ** 20
V7X_VMEM_BYTES = 64 * MIB


def _params(sem, vmem_mib):
    assert vmem_mib * MIB < V7X_VMEM_BYTES
    return pltpu.CompilerParams(dimension_semantics=sem, vmem_limit_bytes=vmem_mib * MIB)


def _pow2_tile(limit, *sizes):
    t = limit
    while any(s % t for s in sizes):
        t //= 2
    return t


class _Geo:
    def __init__(self, b, s, c):
        self.b, self.s, self.c = b, s, c
        self.n_lat, self.n_ctx = b * s, b * c
        self.m = self.n_lat + self.n_ctx

    def mod_row(self, i, tm):
        return jnp.where(i < self.n_lat // tm, i // (self.s // tm), self.b)


def _adaln_kernel(c_ref, w_ref, b_ref, o_ref):
    c = c_ref[...]
    s = c * jax.nn.sigmoid(c)
    o_ref[0] = jnp.dot(s.astype(BF16), w_ref[0].astype(BF16), preferred_element_type=F32) + b_ref[0]


def _adaln_mods(cond, ada_w, ada_b):
    n_l, d, n = ada_w.shape
    r = cond.shape[0]
    tn = 1024
    return pl.pallas_call(
        _adaln_kernel,
        grid=(n_l, n // tn),
        in_specs=[
            pl.BlockSpec((r, d), lambda l, j: (0, 0)),
            pl.BlockSpec((1, d, tn), lambda l, j: (l, 0, j)),
            pl.BlockSpec((1, 1, tn), lambda l, j: (l, 0, j)),
        ],
        out_specs=pl.BlockSpec((1, r, tn), lambda l, j: (l, 0, j)),
        out_shape=jax.ShapeDtypeStruct((n_l, r, n), F32),
        compiler_params=_params(("arbitrary", "arbitrary"), 40),
        name="adaln",
    )(cond, ada_w, ada_b.reshape(n_l, 1, n))


def _modulated_norm(x, g, shift, scale):
    y = x * lax.rsqrt(jnp.mean(x * x, axis=-1, keepdims=True) + EPS)
    return y * g * (1.0 + scale) + shift


def _norm_proj_kernel(x_ref, g_ref, sh_ref, sc_ref, w_ref, o_ref, h_scr):
    @pl.when(pl.program_id(1) == 0)
    def _():
        h_scr[...] = _modulated_norm(x_ref[...], g_ref[...], sh_ref[0], sc_ref[0]).astype(BF16)

    o_ref[...] = jnp.dot(h_scr[...], w_ref[...], preferred_element_type=F32).astype(o_ref.dtype)


def _norm_proj(xa, geo, g, mod, w, out_dtype, tn):
    d = xa.shape[1]
    n = w.shape[1]
    tm = _pow2_tile(1024, geo.s, geo.n_ctx)
    row = lambda i, j: (geo.mod_row(i, tm), 0, 0)
    row_scale = lambda i, j: (geo.mod_row(i, tm), 0, 1)
    return pl.pallas_call(
        _norm_proj_kernel,
        grid=(geo.m // tm, n // tn),
        in_specs=[
            pl.BlockSpec((tm, d), lambda i, j: (i, 0)),
            pl.BlockSpec((1, d), lambda i, j: (0, 0)),
            pl.BlockSpec((1, 1, d), row),
            pl.BlockSpec((1, 1, d), row_scale),
            pl.BlockSpec((d, tn), lambda i, j: (0, j)),
        ],
        out_specs=pl.BlockSpec((tm, tn), lambda i, j: (i, j)),
        out_shape=jax.ShapeDtypeStruct((geo.m, n), out_dtype),
        scratch_shapes=[pltpu.VMEM((tm, d), BF16)],
        compiler_params=_params(("arbitrary", "arbitrary"), 48),
        name="norm_proj",
    )(xa, g.reshape(1, d), mod, mod, w)


def _qk_prep_kernel(t_ref, g_ref, cos_ref, sin_ref, o_ref):
    t = t_ref[...].astype(F32)
    lane = lax.broadcasted_iota(jnp.int32, t.shape, 1)
    first = lane < DA_QK_DIM
    sq = t * t
    sa = jnp.sum(jnp.where(first, sq, 0.0), axis=-1, keepdims=True)
    sb = jnp.sum(jnp.where(first, 0.0, sq), axis=-1, keepdims=True)
    ms = jnp.where(first, sa, sb) * (1.0 / DA_QK_DIM)
    y = t * lax.rsqrt(ms + EPS) * g_ref[0]
    partner = jnp.where((lane & ROPE_FREQS) == 0,
                        pltpu.roll(y, LANES - ROPE_FREQS, 1), pltpu.roll(y, ROPE_FREQS, 1))
    o_ref[...] = (y * cos_ref[...] + partner * sin_ref[...]).astype(BF16)


def _rope_tables(geo, tm):
    pos = jnp.arange(geo.s)
    inv = ROPE_BASE ** (-jnp.arange(ROPE_FREQS, dtype=F32) * 2.0 / ROPE_AXIS_DIM)
    ang = jnp.stack([pos // GRID_W, pos % GRID_W], axis=-1).astype(F32)[..., None] * inv
    cos, sin = jnp.cos(ang), jnp.sin(ang)
    cos64 = jnp.concatenate([cos, cos], axis=-1).reshape(geo.s, DA_QK_DIM)
    sin64 = jnp.concatenate([-sin, sin], axis=-1).reshape(geo.s, DA_QK_DIM)
    cos_t = jnp.concatenate([jnp.tile(cos64, (1, 2)), jnp.ones((tm, HEAD_DIM), F32)], axis=0)
    sin_t = jnp.concatenate([jnp.tile(sin64, (1, 2)), jnp.zeros((tm, HEAD_DIM), F32)], axis=0)
    return cos_t, sin_t


def _qk_prep(proj, geo, gq, gk):
    tm = _pow2_tile(512, geo.s, geo.n_ctx)
    cos_t, sin_t = _rope_tables(geo, tm)
    scale = DA_QK_DIM ** -0.5 * math.log2(math.e)
    gains = jnp.stack([jnp.tile(gq.astype(F32) * scale, 2), jnp.tile(gk.astype(F32), 2)]).reshape(2, 1, HEAD_DIM)
    n_lat_tiles, per_seq = geo.n_lat // tm, geo.s // tm
    tab = lambda i, j: (jnp.where(i < n_lat_tiles, i % per_seq, per_seq), 0)
    n_blocks = (DA_Q_W * 2) // HEAD_DIM
    return pl.pallas_call(
        _qk_prep_kernel,
        grid=(geo.m // tm, n_blocks),
        in_specs=[
            pl.BlockSpec((tm, HEAD_DIM), lambda i, j: (i, j)),
            pl.BlockSpec((1, 1, HEAD_DIM), lambda i, j: (j // DA_HEADS, 0, 0)),
            pl.BlockSpec((tm, HEAD_DIM), tab),
            pl.BlockSpec((tm, HEAD_DIM), tab),
        ],
        out_specs=pl.BlockSpec((tm, HEAD_DIM), lambda i, j: (i, j)),
        out_shape=jax.ShapeDtypeStruct((geo.m, 2 * DA_Q_W), BF16),
        compiler_params=_params(("arbitrary", "arbitrary"), 32),
        name="qk_prep",
    )(proj, gains, cos_t, sin_t)


def _fori_unrolled(n, unroll, body, init):
    def outer(j, carry):
        for u in range(unroll):
            carry = body(j * unroll + u, carry)
        return carry
    return lax.fori_loop(0, n // unroll, outer, init)


def _fold_lanes(x, op):
    out = x[:, :LANES]
    for j in range(1, x.shape[1] // LANES):
        out = op(out, x[:, j * LANES:(j + 1) * LANES])
    return out


def _attn_kernel(lam_ref, q_ref, kl_ref, kc_ref, vl_ref, vc_ref, g_ref, o_ref, s_lat, s_ctx, *, tk, n_lat_chunks,
                 nq_lat, sub_scale):
    q = q_ref[...]
    tq = q.shape[0]
    lane = lax.broadcasted_iota(jnp.int32, q.shape, 1)
    zero = jnp.zeros_like(q)
    qs = (jnp.where(lane < DA_QK_DIM, q, zero), jnp.where(lane < DA_QK_DIM, zero, q))
    nt = (((1,), (1,)), ((), ()))
    n_chunks = jnp.where(pl.program_id(2) < nq_lat, n_lat_chunks, 0)

    def scores(c, m):
        k = kl_ref[pl.ds(pl.multiple_of(c * tk, tk), tk), :]
        new = []
        for i in range(2):
            s = lax.dot_general(qs[i], k, nt, preferred_element_type=F32)
            s_lat[i, c] = s
            new.append(jnp.maximum(m[i], _fold_lanes(s, jnp.maximum)))
        return tuple(new)

    neg = jnp.full((tq, LANES), -jnp.inf, F32)
    m_lat = _fori_unrolled(n_chunks, math.gcd(n_lat_chunks, 4), scores, (neg, neg))
    m = []
    for i in range(2):
        s = lax.dot_general(qs[i], kc_ref[...], nt, preferred_element_type=F32)
        s_ctx[i] = s
        m.append(jnp.max(jnp.maximum(m_lat[i], _fold_lanes(s, jnp.maximum)), axis=-1, keepdims=True))

    def weighted(c, carry):
        v = vl_ref[pl.ds(pl.multiple_of(c * tk, tk), tk), :]
        new = []
        for i in range(2):
            p = jnp.exp2(s_lat[i, c] - m[i])
            new.append(carry[2 * i] + _fold_lanes(p, jnp.add))
            new.append(carry[2 * i + 1] + jnp.dot(p.astype(BF16), v, preferred_element_type=F32))
        return tuple(new)

    z = jnp.zeros((tq, LANES), F32)
    part = _fori_unrolled(n_chunks, math.gcd(n_lat_chunks, 4), weighted, (z, z, z, z))
    outs = []
    for i in range(2):
        p = jnp.exp2(s_ctx[i] - m[i])
        l = jnp.sum(part[2 * i] + _fold_lanes(p, jnp.add), axis=-1, keepdims=True)
        outs.append((part[2 * i + 1] + jnp.dot(p.astype(BF16), vc_ref[...], preferred_element_type=F32)) / l)
    o = outs[0] - lam_ref[0, 0] * outs[1]
    y = o * lax.rsqrt(jnp.mean(o * o, axis=-1, keepdims=True) + EPS) * g_ref[...]
    o_ref[...] = (y * sub_scale).astype(BF16)


def _diff_attention(qk, proj, geo, lam, subln_g, lam_init):
    tq = _pow2_tile(256, geo.s, geo.c)
    tk = _pow2_tile(512, geo.s)
    nq_lat, nq_ctx = geo.s // tq, geo.c // tq
    k_col0, v_col0 = DA_Q_W // HEAD_DIM, 2 * DA_Q_W // HEAD_DIM
    ctx_blk0 = geo.n_lat // geo.c

    def q_idx(b, h, qi, col0=0):
        lat = b * nq_lat + qi
        ctx = geo.n_lat // tq + b * nq_ctx + (qi - nq_lat)
        return (jnp.where(qi < nq_lat, lat, ctx), col0 + h)

    kernel = functools.partial(_attn_kernel, tk=tk, n_lat_chunks=geo.s // tk, nq_lat=nq_lat,
                               sub_scale=1.0 - lam_init)
    return pl.pallas_call(
        kernel,
        grid=(geo.b, DA_HEADS, nq_lat + nq_ctx),
        in_specs=[
            pl.BlockSpec(memory_space=pltpu.SMEM),
            pl.BlockSpec((tq, HEAD_DIM), q_idx),
            pl.BlockSpec((geo.s, HEAD_DIM), lambda b, h, qi: (b, k_col0 + h)),
            pl.BlockSpec((geo.c, HEAD_DIM), lambda b, h, qi: (ctx_blk0 + b, k_col0 + h)),
            pl.BlockSpec((geo.s, HEAD_DIM), lambda b, h, qi: (b, v_col0 + h)),
            pl.BlockSpec((geo.c, HEAD_DIM), lambda b, h, qi: (ctx_blk0 + b, v_col0 + h)),
            pl.BlockSpec((1, HEAD_DIM), lambda b, h, qi: (0, 0)),
        ],
        out_specs=pl.BlockSpec((tq, HEAD_DIM), q_idx),
        out_shape=jax.ShapeDtypeStruct((geo.m, DA_V_W), BF16),
        scratch_shapes=[pltpu.VMEM((2, geo.s // tk, tq, tk), F32), pltpu.VMEM((2, tq, geo.c), F32)],
        compiler_params=_params(("arbitrary", "arbitrary", "arbitrary"), 40),
        name="diff_attn",
    )(lam.reshape(1, 1), qk, qk, qk, proj, proj, subln_g.reshape(1, HEAD_DIM).astype(F32))


def _gelu(x):
    return 0.5 * x * (1.0 + lax.erf(x * (1.0 / math.sqrt(2.0))))


def _gmlp_kernel(u_ref, v_ref, lng_ref, lnb_ref, ws_ref, bs_ref, o_ref):
    tm = u_ref.shape[0]
    for r in range(tm // GM_CHUNK):
        rows = slice(r * GM_CHUNK, (r + 1) * GM_CHUNK)
        for g in range(GM_GROUPS):
            cols = slice(g * HEAD_DIM, (g + 1) * HEAD_DIM)
            v = _gelu(v_ref[rows, cols].astype(F32))
            mu = jnp.mean(v, axis=-1, keepdims=True)
            var = jnp.mean(jnp.square(v - mu), axis=-1, keepdims=True)
            vn = (v - mu) * lax.rsqrt(var + EPS) * lng_ref[:, cols] + lnb_ref[:, cols]
            s = jnp.dot(ws_ref[g], vn.astype(BF16), preferred_element_type=F32) + bs_ref[:, cols]
            o_ref[rows, cols] = (_gelu(u_ref[rows, cols].astype(F32)) * s).astype(BF16)


def _gmlp(proj, geo, ln_g, ln_b, w_s, b_s):
    tm = _pow2_tile(256, geo.s, geo.c)
    u_blk = (2 * DA_Q_W + DA_V_W) // GM_W
    bs_full = jnp.repeat(b_s.astype(F32).T, HEAD_DIM, axis=1)
    return pl.pallas_call(
        _gmlp_kernel,
        grid=(geo.m // tm,),
        in_specs=[
            pl.BlockSpec((tm, GM_W), lambda i: (i, u_blk)),
            pl.BlockSpec((tm, GM_W), lambda i: (i, u_blk + 1)),
            pl.BlockSpec((1, GM_W), lambda i: (0, 0)),
            pl.BlockSpec((1, GM_W), lambda i: (0, 0)),
            pl.BlockSpec((GM_GROUPS, GM_CHUNK, GM_CHUNK), lambda i: (0, 0, 0)),
            pl.BlockSpec((GM_CHUNK, GM_W), lambda i: (0, 0)),
        ],
        out_specs=pl.BlockSpec((tm, GM_W), lambda i: (i, 0)),
        out_shape=jax.ShapeDtypeStruct((geo.m, GM_W), BF16),
        compiler_params=_params(("arbitrary",), 32),
        name="gmlp",
    )(proj, proj, ln_g.reshape(1, GM_W).astype(F32), ln_b.reshape(1, GM_W).astype(F32), w_s.astype(BF16), bs_full)


def _pack_bf16_pair(lo, hi):
    lo_bits = lax.bitcast_convert_type(lo.astype(BF16).astype(F32), U32)
    hi_bits = lax.bitcast_convert_type(hi.astype(BF16).astype(F32), U32)
    return (lo_bits >> 16) | (hi_bits & jnp.uint32(0xFFFF0000))


def _resid_norm_route(y, x_ref, gate_ref, g_ref, sh_ref, sc_ref, rw_ref, rb_ref, xn_ref, hp_ref, lg_ref):
    xn = x_ref[...] + gate_ref[0] * y
    xn_ref[...] = xn
    h = _modulated_norm(xn, g_ref[...], sh_ref[0], sc_ref[0])
    lg_ref[...] = jnp.dot(h, rw_ref[...], preferred_element_type=F32, precision=HIGHEST) + rb_ref[...]
    half = h.shape[1] // 2
    hp_ref[...] = _pack_bf16_pair(h[:, :half], h[:, half:])


def _epilogue_specs(geo, tm, d, grid_rank):
    def ix(f):
        return (lambda i: f(i)) if grid_rank == 1 else (lambda i, k: f(i))
    mod = lambda k: ix(lambda i: (geo.mod_row(i, tm), 0, k))
    in_specs = [
        pl.BlockSpec((tm, d), ix(lambda i: (i, 0))),
        pl.BlockSpec((1, 1, d), mod(2)),
        pl.BlockSpec((1, d), ix(lambda i: (0, 0))),
        pl.BlockSpec((1, 1, d), mod(3)),
        pl.BlockSpec((1, 1, d), mod(4)),
        pl.BlockSpec((d, ROUTER_PAD), ix(lambda i: (0, 0))),
        pl.BlockSpec((1, ROUTER_PAD), ix(lambda i: (0, 0))),
    ]
    out_specs = [
        pl.BlockSpec((tm, d), ix(lambda i: (i, 0))),
        pl.BlockSpec((tm, d // 2), ix(lambda i: (i, 0))),
        pl.BlockSpec((tm, ROUTER_PAD), ix(lambda i: (i, 0))),
    ]
    return in_specs, out_specs


def _epilogue_out_shapes(n_rows, d):
    return [jax.ShapeDtypeStruct((n_rows, d), F32), jax.ShapeDtypeStruct((n_rows, d // 2), U32),
            jax.ShapeDtypeStruct((n_rows, ROUTER_PAD), F32)]


def _router_operands(router_w, router_b):
    d = router_w.shape[0]
    rw = jnp.zeros((d, ROUTER_PAD), F32).at[:, :N_EXPERTS].set(router_w.astype(F32))
    rb = jnp.zeros((1, ROUTER_PAD), F32).at[0, :N_EXPERTS].set(router_b.astype(F32))
    return rw, rb


def _attn_out_kernel(a_ref, gm_ref, wo_ref, x_ref, gate_ref, g_ref, sh_ref, sc_ref, rw_ref, rb_ref,
                     xn_ref, hp_ref, lg_ref):
    y = jnp.dot(a_ref[...], wo_ref[:DA_V_W, :], preferred_element_type=F32)
    y = y + jnp.dot(gm_ref[...], wo_ref[DA_V_W:, :], preferred_element_type=F32)
    _resid_norm_route(y, x_ref, gate_ref, g_ref, sh_ref, sc_ref, rw_ref, rb_ref, xn_ref, hp_ref, lg_ref)


def _attn_out(attn, gm, w_o, xa, geo, mod, g2, rw, rb):
    d = xa.shape[1]
    tm = _pow2_tile(512, geo.s, geo.n_ctx)
    ep_in, ep_out = _epilogue_specs(geo, tm, d, 1)
    return pl.pallas_call(
        _attn_out_kernel,
        grid=(geo.m // tm,),
        in_specs=[
            pl.BlockSpec((tm, DA_V_W), lambda i: (i, 0)),
            pl.BlockSpec((tm, GM_W), lambda i: (i, 0)),
            pl.BlockSpec((DA_V_W + GM_W, d), lambda i: (0, 0)),
        ] + ep_in,
        out_specs=ep_out,
        out_shape=_epilogue_out_shapes(geo.m, d),
        compiler_params=_params(("arbitrary",), 52),
        name="attn_out",
    )(attn, gm, w_o, xa, mod, g2.reshape(1, d), mod, mod, rw, rb)


def _route(logits, n_tok):
    top_val, top_idx = lax.top_k(logits[:, :N_EXPERTS], TOP_K)
    gates = jax.nn.softmax(top_val, axis=-1)
    n_asg = n_tok * TOP_K
    n_tiles = n_asg // MOE_TM + N_EXPERTS
    flat_e = top_idx.reshape(-1).astype(jnp.int32)
    order = jnp.argsort(flat_e, stable=True).astype(jnp.int32)
    experts = jnp.arange(N_EXPERTS, dtype=jnp.int32)
    counts = jnp.sum((flat_e[:, None] == experts[None, :]).astype(jnp.int32), axis=0)
    grp_start = jnp.cumsum(counts) - counts
    tiles_e = (counts + MOE_TM - 1) // MOE_TM
    tile_end = jnp.cumsum(tiles_e)
    tile_start = tile_end - tiles_e
    n_valid = tile_end[-1]
    t = jnp.arange(n_tiles, dtype=jnp.int32)
    t_eff = jnp.minimum(t, n_valid - 1)
    tile_e = jnp.minimum(jnp.searchsorted(tile_end, t_eff, side="right"), N_EXPERTS - 1).astype(jnp.int32)
    tile_off = (t_eff - tile_start[tile_e]) * MOE_TM
    tile_rows = jnp.where(t < n_valid, jnp.clip(counts[tile_e] - tile_off, 0, MOE_TM), 0).astype(jnp.int32)
    r = jnp.arange(MOE_TM, dtype=jnp.int32)[None, :]
    order_pad = jnp.concatenate([order, jnp.zeros((MOE_TM,), jnp.int32)])
    seg = jax.vmap(lambda s0: lax.dynamic_slice(order_pad, (s0,), (MOE_TM,)))(grp_start[tile_e] + tile_off)
    asg = jnp.where(r < tile_rows[:, None], seg, 0)
    row_tok = (asg // TOP_K).reshape(n_tiles, 1, MOE_TM)
    row_slot = ((asg % TOP_K) * n_tok + asg // TOP_K).reshape(n_tiles, 1, MOE_TM)
    return tile_e, tile_rows, row_tok, row_slot, gates


def _unpack_bf16_pair(u):
    return lax.bitcast_convert_type(u << 16, F32), lax.bitcast_convert_type(u & jnp.uint32(0xFFFF0000), F32)


def _row_blocks(n_rows):
    return (n_rows + MOE_RB - 1) // MOE_RB


def _padded_blocks(n_blocks):
    return (n_blocks + MOE_P1_BLOCKS - 1) // MOE_P1_BLOCKS * MOE_P1_BLOCKS


def _moe_kernel(te_ref, tr_ref, tok_ref, tok_next_ref, slot_ref, hp_hbm, w1g_ref, w1l_ref, b1g_ref, b1l_ref,
                w2_ref, b2_ref, ys_hbm, xbuf, xs, act, w2b, ystage, gsem, ssem):
    t, s = pl.program_id(0), pl.program_id(1)
    n_tiles = pl.num_programs(0)
    rows = tr_ref[t]
    nrb = _row_blocks(rows)
    nf = act.shape[0]
    half = xbuf.shape[2]
    grp = MOE_RB // SUBLANES
    prefetched = jnp.where(t > 0, _padded_blocks(_row_blocks(tr_ref[jnp.maximum(t - 1, 0)])), 0)

    def gather_start(rows_ref, r_dyn, u):
        pltpu.make_async_copy(hp_hbm.at[pl.ds(rows_ref[0, 0, r_dyn * SUBLANES + u], 1), :],
                              xbuf.at[r_dyn, pl.ds(u, 1), :], gsem.at[0]).start()

    def gather_wait(n_blocks):
        def wait_block(i, _):
            blk = xbuf.at[pl.ds(pl.multiple_of(i * grp, grp), grp)]
            pltpu.make_async_copy(blk, blk, gsem.at[0]).wait()
            return 0
        lax.fori_loop(0, n_blocks, wait_block, 0)

    @pl.when(s == 0)
    def _():
        def top_up(j, _):
            for u in range(SUBLANES):
                gather_start(tok_ref, j, u)
            return 0
        lax.fori_loop(prefetched * grp, nrb * grp, top_up, 0)
        gather_wait(jnp.maximum(prefetched, nrb))

        def unpack_block(i, _):
            r0 = pl.multiple_of(i * MOE_RB, MOE_RB)
            u32 = xbuf[pl.ds(pl.multiple_of(i * grp, grp), grp)].reshape(MOE_RB, half)
            lo, hi = _unpack_bf16_pair(u32)
            xs[pl.ds(r0, MOE_RB), :half] = lo.astype(BF16)
            xs[pl.ds(r0, MOE_RB), half:] = hi.astype(BF16)
            return 0
        lax.fori_loop(0, nrb, unpack_block, 0)

        def zero_block(i, _):
            xs[pl.ds(pl.multiple_of(i * MOE_RB, MOE_RB), MOE_RB), :] = jnp.zeros((MOE_RB, xs.shape[1]), BF16)
            return 0
        lax.fori_loop(nrb, _padded_blocks(nrb), zero_block, 0)

    @pl.when(jnp.logical_and(s < nf, nrb > 0))
    def _():
        w1g = w1g_ref[0, 0].astype(BF16)
        w1l = w1l_ref[0, 0].astype(BF16)
        w2b[pl.ds(pl.multiple_of(s * MOE_TF, MOE_TF), MOE_TF), :] = w2_ref[0, 0].astype(BF16)

        n_padded = _padded_blocks(nrb)
        share = grp // nf

        def iteration(it, _):
            for b in range(MOE_P1_BLOCKS):
                i = it * MOE_P1_BLOCKS + b
                r0 = pl.multiple_of(i * MOE_RB, MOE_RB)
                g0 = (s * n_padded + i) * share
                for j in range(share):
                    for u in range(SUBLANES):
                        gather_start(tok_next_ref, g0 + j, u)
                x = xs[pl.ds(r0, MOE_RB), :]
                g = jnp.minimum(jnp.dot(x, w1g, preferred_element_type=F32) + b1g_ref[0, 0], SWIGLU_LIMIT)
                lin = jnp.clip(jnp.dot(x, w1l, preferred_element_type=F32) + b1l_ref[0, 0],
                               -SWIGLU_LIMIT, SWIGLU_LIMIT)
                act[s, pl.ds(r0, MOE_RB), :] = (g * jax.nn.sigmoid(SWIGLU_ALPHA * g) * (lin + 1.0)).astype(BF16)
            return 0
        lax.fori_loop(0, n_padded // MOE_P1_BLOCKS, iteration, 0)

    def scatter_start(stage_slot, r0, g, u):
        pltpu.make_async_copy(ystage.at[stage_slot, g, pl.ds(u, 1), :],
                              ys_hbm.at[pl.ds(slot_ref[0, 0, r0 + g * SUBLANES + u], 1), :],
                              ssem.at[stage_slot]).start()

    def scatter_wait(stage_slot, n_rows):
        if n_rows == MOE_RB:
            blk = ystage.at[stage_slot]
        else:
            blk = ystage.at[stage_slot, 0, pl.ds(0, n_rows), :]
        pltpu.make_async_copy(blk, blk, ssem.at[stage_slot]).wait()

    @pl.when(jnp.logical_and(s == nf, nrb > 0))
    def _():
        def staged(i):
            r0 = pl.multiple_of(i * MOE_RB, MOE_RB)
            a = jnp.concatenate([act[f, pl.ds(r0, MOE_RB), :] for f in range(nf)], axis=1)
            y = jnp.dot(a, w2b[...], preferred_element_type=F32) + b2_ref[0, 0]
            return _pack_bf16_pair(y[:, :half], y[:, half:]).reshape(grp, SUBLANES, half)

        ystage[0] = staged(0)

        def block(i, _):
            stage_slot = i % 2
            r_prev = pl.multiple_of((i - 1) * MOE_RB, MOE_RB)
            for g in range(grp):
                for u in range(SUBLANES):
                    scatter_start(1 - stage_slot, r_prev, g, u)
            packed = staged(i)

            @pl.when(i >= 2)
            def _():
                scatter_wait(stage_slot, MOE_RB)

            ystage[stage_slot] = packed
            return 0
        lax.fori_loop(1, nrb, block, 0)

        last = nrb - 1
        last_slot = last % 2
        r_last = pl.multiple_of(last * MOE_RB, MOE_RB)
        n_last = rows - r_last

        def issue_group(g, _):
            for u in range(SUBLANES):
                scatter_start(last_slot, r_last, g, u)
            return 0
        lax.fori_loop(0, n_last // SUBLANES, issue_group, 0)
        for u in range(SUBLANES - 1):
            @pl.when(u < n_last % SUBLANES)
            def _():
                scatter_start(last_slot, r_last, n_last // SUBLANES, u)

        @pl.when(nrb >= 2)
        def _():
            scatter_wait(1 - last_slot, MOE_RB)

        def wait_one(r, _):
            scatter_wait(last_slot, 1)
            return 0
        lax.fori_loop(0, n_last, wait_one, 0)

    @pl.when(jnp.logical_and(t == n_tiles - 1, s == nf))
    def _():
        gather_wait(_padded_blocks(nrb))


def _moe_experts(hp, n_tok, tile_e, tile_rows, row_tok, row_slot, w1, b1, w2, b2, li):
    n_tiles = row_tok.shape[0]
    d = w2.shape[-1]
    nf = D_EXPERT // MOE_TF
    fz = lambda s, tr, t: jnp.where(tr[t] > 0, jnp.minimum(s, nf - 1), nf - 1)
    b1r = b1.reshape(b1.shape[0], N_EXPERTS, 1, 2 * D_EXPERT)
    b2r = b2.reshape(b2.shape[0], N_EXPERTS, 1, d)
    rows_spec = lambda f: pl.BlockSpec((1, 1, MOE_TM), f, memory_space=pltpu.SMEM)
    grid_spec = pltpu.PrefetchScalarGridSpec(
        num_scalar_prefetch=2,
        grid=(n_tiles, nf + 1),
        in_specs=[
            rows_spec(lambda t, s, te, tr: (t, 0, 0)),
            rows_spec(lambda t, s, te, tr: (jnp.minimum(t + 1, n_tiles - 1), 0, 0)),
            rows_spec(lambda t, s, te, tr: (t, 0, 0)),
            pl.BlockSpec(memory_space=pl.ANY),
            pl.BlockSpec((1, 1, d, MOE_TF), lambda t, s, te, tr: (li, te[t], 0, fz(s, tr, t))),
            pl.BlockSpec((1, 1, d, MOE_TF), lambda t, s, te, tr: (li, te[t], 0, nf + fz(s, tr, t))),
            pl.BlockSpec((1, 1, 1, MOE_TF), lambda t, s, te, tr: (li, te[t], 0, fz(s, tr, t))),
            pl.BlockSpec((1, 1, 1, MOE_TF), lambda t, s, te, tr: (li, te[t], 0, nf + fz(s, tr, t))),
            pl.BlockSpec((1, 1, MOE_TF, d), lambda t, s, te, tr: (li, te[t], fz(s, tr, t), 0)),
            pl.BlockSpec((1, 1, 1, d), lambda t, s, te, tr: (li, te[t], 0, 0)),
        ],
        out_specs=pl.BlockSpec(memory_space=pl.ANY),
        scratch_shapes=[
            pltpu.VMEM((MOE_TM // SUBLANES, SUBLANES, d // 2), U32),
            pltpu.VMEM((MOE_TM, d), BF16),
            pltpu.VMEM((nf, MOE_TM, MOE_TF), BF16),
            pltpu.VMEM((D_EXPERT, d), BF16),
            pltpu.VMEM((2, MOE_RB // SUBLANES, SUBLANES, d // 2), U32),
            pltpu.SemaphoreType.DMA((1,)),
            pltpu.SemaphoreType.DMA((2,)),
        ],
    )
    return pl.pallas_call(
        _moe_kernel,
        grid_spec=grid_spec,
        out_shape=jax.ShapeDtypeStruct((TOP_K * n_tok, d // 2), U32),
        compiler_params=_params(("arbitrary", "arbitrary"), 60),
        name="moe_experts",
    )(tile_e, tile_rows, row_tok, row_tok, row_slot, hp, w1, w1, b1r, b1r, w2, b2r)


def _combine_kernel(ys_ref, gate_ref, x_ref, g5_ref, o_ref):
    half = ys_ref.shape[2]
    gate = gate_ref[...]
    acc_lo = acc_hi = None
    for k in range(TOP_K):
        lo, hi = _unpack_bf16_pair(ys_ref[k])
        g = gate[:, k:k + 1]
        acc_lo = g * lo if k == 0 else acc_lo + g * lo
        acc_hi = g * hi if k == 0 else acc_hi + g * hi
    o_ref[:, :half] = x_ref[:, :half] + g5_ref[0, :, :half] * acc_lo
    o_ref[:, half:] = x_ref[:, half:] + g5_ref[0, :, half:] * acc_hi


def _moe_combine(ys, gates, xn, geo, n_tok, mod):
    d = xn.shape[1]
    tc = _pow2_tile(COMBINE_TC, geo.s, geo.n_ctx)
    return pl.pallas_call(
        _combine_kernel,
        grid=(n_tok // tc,),
        in_specs=[
            pl.BlockSpec((TOP_K, tc, d // 2), lambda i: (0, i, 0)),
            pl.BlockSpec((tc, TOP_K), lambda i: (i, 0)),
            pl.BlockSpec((tc, d), lambda i: (i, 0)),
            pl.BlockSpec((1, 1, d), lambda i: (geo.mod_row(i, tc), 0, 5)),
        ],
        out_specs=pl.BlockSpec((tc, d), lambda i: (i, 0)),
        out_shape=jax.ShapeDtypeStruct((n_tok, d), F32),
        compiler_params=_params(("arbitrary",), 40),
        name="moe_combine",
    )(ys.reshape(TOP_K, n_tok, d // 2), gates, xn, mod)


def _moe_layer(xn, hp, logits, geo, n_tok, mod, w1, b1, w2, b2, li):
    tile_e, tile_rows, row_tok, row_slot, gates = _route(logits[:n_tok], n_tok)
    ys = _moe_experts(hp, n_tok, tile_e, tile_rows, row_tok, row_slot, w1, b1, w2, b2, li)
    return _moe_combine(ys, gates, xn, geo, n_tok, mod)


def _ssd_prep_kernel(raw_ref, bias_ref, aneg_ref, dt_ref, ac_ref):
    tm = raw_ref.shape[0]
    li = lax.broadcasted_iota(jnp.int32, (SSD_CHUNK, SSD_CHUNK), 0)
    si = lax.broadcasted_iota(jnp.int32, (SSD_CHUNK, SSD_CHUNK), 1)
    tri_f = (si <= li).astype(F32)
    tri_b = (si >= li).astype(F32)
    lane = lax.broadcasted_iota(jnp.int32, (SSD_CHUNK, SSD_DT_W), 1)
    for c in range(tm // SSD_CHUNK):
        rows = slice(c * SSD_CHUNK, (c + 1) * SSD_CHUNK)
        v = raw_ref[rows, :] + bias_ref[...]
        dt = jnp.maximum(v, 0.0) + jnp.log1p(jnp.exp(-jnp.abs(v)))
        a = dt * aneg_ref[...]
        fwd = jnp.dot(tri_f, a, preferred_element_type=F32, precision=HIGHEST)
        bwd = jnp.dot(tri_b, a, preferred_element_type=F32, precision=HIGHEST)
        dt_ref[rows, :] = dt
        ac_ref[rows, :] = jnp.where(lane < SSD_HEADS, fwd, bwd)


def _ssd_prep(dt_raw, geo, dt_bias, a_log):
    tm = _pow2_tile(512, geo.s, geo.c)
    a_neg = -jnp.exp(a_log.astype(F32)).reshape(1, SSD_DT_W)
    spec = pl.BlockSpec((tm, SSD_DT_W), lambda i: (i, 0))
    vec = pl.BlockSpec((1, SSD_DT_W), lambda i: (0, 0))
    return pl.pallas_call(
        _ssd_prep_kernel,
        grid=(geo.m // tm,),
        in_specs=[spec, vec, vec],
        out_specs=[spec, spec],
        out_shape=[jax.ShapeDtypeStruct((geo.m, SSD_DT_W), F32)] * 2,
        compiler_params=_params(("arbitrary",), 32),
        name="ssd_prep",
    )(dt_raw, dt_bias.astype(F32).reshape(1, SSD_DT_W), a_neg)


CONV_HALO = 16


def _conv_kernel(prev_ref, cur_ref, next_ref, w_ref, b_ref, o_ref, ext, *, geo, tm):
    i = pl.program_id(0)
    n_lat_tiles = geo.n_lat // tm
    per_lat, per_ctx = geo.s // tm, geo.c // tm
    lat = i < n_lat_tiles
    p = jnp.where(lat, i % per_lat, (i - n_lat_tiles) % per_ctx)
    first = p == 0
    last = jnp.where(lat, p == per_lat - 1, p == per_ctx - 1)
    ext[0:CONV_HALO, :] = jnp.where(first, 0.0, prev_ref[...].astype(F32))
    ext[CONV_HALO:CONV_HALO + tm, :] = cur_ref[...].astype(F32)
    ext[CONV_HALO + tm:, :] = jnp.where(last, 0.0, next_ref[...].astype(F32))
    pad = SSD_CONV_W // 2
    acc = w_ref[0:1, :] * ext[pl.ds(CONV_HALO - pad, tm), :]
    for k in range(1, SSD_CONV_W):
        acc = acc + w_ref[k:k + 1, :] * ext[pl.ds(CONV_HALO - pad + k, tm), :]
    acc = acc + b_ref[...]
    o_ref[...] = (acc * jax.nn.sigmoid(acc)).astype(BF16)


def _conv_silu(zx, geo, conv_w, conv_b):
    tm = _pow2_tile(256, geo.s, geo.c)
    tc = 1024
    col0 = SSD_D_INNER // tc
    hb = tm // CONV_HALO
    n_halo_blocks = geo.m // CONV_HALO
    return pl.pallas_call(
        functools.partial(_conv_kernel, geo=geo, tm=tm),
        grid=(geo.m // tm, SSD_CONV_CH // tc),
        in_specs=[
            pl.BlockSpec((CONV_HALO, tc), lambda i, j: (jnp.maximum(i * hb - 1, 0), col0 + j)),
            pl.BlockSpec((tm, tc), lambda i, j: (i, col0 + j)),
            pl.BlockSpec((CONV_HALO, tc), lambda i, j: (jnp.minimum((i + 1) * hb, n_halo_blocks - 1), col0 + j)),
            pl.BlockSpec((SSD_CONV_W, tc), lambda i, j: (0, j)),
            pl.BlockSpec((1, tc), lambda i, j: (0, j)),
        ],
        out_specs=pl.BlockSpec((tm, tc), lambda i, j: (i, j)),
        out_shape=jax.ShapeDtypeStruct((geo.m, SSD_CONV_CH), BF16),
        scratch_shapes=[pltpu.VMEM((tm + 2 * CONV_HALO, tc), F32)],
        compiler_params=_params(("arbitrary", "arbitrary"), 32),
        name="ssd_conv",
    )(zx, zx, zx, conv_w.astype(F32), conv_b.astype(F32).reshape(1, SSD_CONV_CH))


def _ssd_scan_kernel(xs_ref, b_ref, c_ref, dt_ref, ac_ref, y_ref, state, xg, bt_g, c_g, dt_g, ac_g, ac_t, y_g):
    d, s = pl.program_id(1), pl.program_id(2)
    q = SSD_CHUNK
    is_fwd = d == 0

    @pl.when(s == 0)
    def _():
        state[...] = jnp.zeros(state.shape, F32)

    dt2, ac2 = dt_ref[...], ac_ref[...]
    dtc = jnp.where(is_fwd, dt2[:, :SSD_HEADS], dt2[:, SSD_HEADS:])
    acc = jnp.where(is_fwd, ac2[:, :SSD_HEADS], ac2[:, SSD_HEADS:])
    ac_t[...] = ac2.T
    b_f32 = b_ref[...].astype(F32)
    for g in range(SSD_GROUPS):
        xg[g] = xs_ref[:, g * SSD_GROUP_W:(g + 1) * SSD_GROUP_W]
        bt_g[g] = b_f32[:, g * SSD_STATE:(g + 1) * SSD_STATE].T.astype(BF16)
        c_g[g] = c_ref[:, g * SSD_STATE:(g + 1) * SSD_STATE]
        dt_g[g] = dtc[:, g * SSD_HPG:(g + 1) * SSD_HPG]
        ac_g[g] = acc[:, g * SSD_HPG:(g + 1) * SSD_HPG]

    li = lax.broadcasted_iota(jnp.int32, (q, q), 0)
    si = lax.broadcasted_iota(jnp.int32, (q, q), 1)
    mask = jnp.where(is_fwd, li - si, si - li) >= 0
    lane = lax.broadcasted_iota(jnp.int32, (q, LANES), 1)
    lo_half = lane < SSD_HEAD_DIM
    lo_row = lo_half[0:1, :]

    def group(g, _):
        cg = c_g[g]
        cb = jnp.dot(cg, bt_g[g], preferred_element_type=F32)
        dtg, acg = dt_g[g], ac_g[g]
        xf = xg[g].astype(F32)
        a_cols, d_cols, decays = [], [], []
        for j in range(SSD_HPG):
            a_col = jnp.broadcast_to(acg[:, j:j + 1], (q, q))
            a_row = ac_t[pl.ds(d * SSD_HEADS + g * SSD_HPG + j, 1), :]
            seg = a_col - a_row
            decays.append((cb * jnp.exp(jnp.where(mask, seg, -jnp.inf))).astype(BF16))
            a_cols.append(a_col)
            d_cols.append(jnp.broadcast_to(dtg[:, j:j + 1], (q, q)))
        for p in range(SSD_HPG // 2):
            cols = slice(p * LANES, (p + 1) * LANES)
            acx = jnp.where(lo_half, a_cols[2 * p], a_cols[2 * p + 1])
            dtx = jnp.where(lo_half, d_cols[2 * p], d_cols[2 * p + 1])
            a_last = jnp.where(is_fwd, acx[q - 1:q, :], acx[0:1, :])
            xdt = xf[:, cols] * dtx
            xdt_b = xdt.astype(BF16)
            zero = jnp.zeros_like(xdt_b)
            rhs = jnp.concatenate([jnp.where(lo_half, xdt_b, zero), jnp.where(lo_half, zero, xdt_b)], axis=0)
            lhs = jnp.concatenate([decays[2 * p], decays[2 * p + 1]], axis=1)
            st = state[g, :, cols]
            y = jnp.dot(lhs, rhs, preferred_element_type=F32)
            y = y + jnp.dot(cg, st.astype(BF16), preferred_element_type=F32) * jnp.exp(acx)
            y_g[g, :, cols] = y
            wx = (xdt * jnp.exp(a_last - acx)).astype(BF16)
            state[g, :, cols] = st * jnp.exp(a_last) + jnp.dot(bt_g[g], wx, preferred_element_type=F32)
        return 0

    lax.fori_loop(0, SSD_GROUPS, group, 0)
    for g in range(SSD_GROUPS):
        y_ref[0, :, g * SSD_GROUP_W:(g + 1) * SSD_GROUP_W] = y_g[g].astype(BF16)


def _ssd_scan(cv, dt, ac, geo):
    q = SSD_CHUNK
    ncc, nlc = geo.c // q, geo.s // q
    bc_col0 = SSD_D_INNER // (SSD_GROUPS * SSD_STATE)

    def lat_chunk(d, s):
        c = jnp.maximum(s - ncc, 0)
        return jnp.where(d == 0, c, nlc - 1 - c)

    def row_blk(b, d, s):
        ctx_c = jnp.where(d == 0, s, ncc - 1 - s)
        return jnp.where(s < ncc, geo.b * nlc + b * ncc + ctx_c, b * nlc + lat_chunk(d, s))

    return pl.pallas_call(
        _ssd_scan_kernel,
        grid=(geo.b, 2, ncc + nlc),
        in_specs=[
            pl.BlockSpec((q, SSD_D_INNER), lambda b, d, s: (row_blk(b, d, s), 0)),
            pl.BlockSpec((q, SSD_GROUPS * SSD_STATE), lambda b, d, s: (row_blk(b, d, s), bc_col0 + 2 * d)),
            pl.BlockSpec((q, SSD_GROUPS * SSD_STATE), lambda b, d, s: (row_blk(b, d, s), bc_col0 + 2 * d + 1)),
            pl.BlockSpec((q, SSD_DT_W), lambda b, d, s: (row_blk(b, d, s), 0)),
            pl.BlockSpec((q, SSD_DT_W), lambda b, d, s: (row_blk(b, d, s), 0)),
        ],
        out_specs=pl.BlockSpec((1, q, SSD_D_INNER), lambda b, d, s: (d, b * nlc + lat_chunk(d, s), 0)),
        out_shape=jax.ShapeDtypeStruct((2, geo.n_lat, SSD_D_INNER), BF16),
        scratch_shapes=[
            pltpu.VMEM((SSD_GROUPS, SSD_STATE, SSD_GROUP_W), F32),
            pltpu.VMEM((SSD_GROUPS, q, SSD_GROUP_W), BF16),
            pltpu.VMEM((SSD_GROUPS, SSD_STATE, q), BF16),
            pltpu.VMEM((SSD_GROUPS, q, SSD_STATE), BF16),
            pltpu.VMEM((SSD_GROUPS, q, SSD_HPG), F32),
            pltpu.VMEM((SSD_GROUPS, q, SSD_HPG), F32),
            pltpu.VMEM((SSD_DT_W, q), F32),
            pltpu.VMEM((SSD_GROUPS, q, SSD_GROUP_W), F32),
        ],
        compiler_params=_params(("arbitrary", "arbitrary", "arbitrary"), 40),
        name="ssd_scan",
    )(cv, cv, cv, dt, ac)


def _ssd_out_kernel(yf_ref, yb_ref, xs_ref, z_ref, skip_ref, ng_ref, wo_ref, x_ref, gate_ref, g_ref, sh_ref,
                    sc_ref, rw_ref, rb_ref, xn_ref, hp_ref, lg_ref, acc):
    k = pl.program_id(1)
    y = yf_ref[0].astype(F32) + yb_ref[0].astype(F32) + skip_ref[...] * xs_ref[...].astype(F32)
    z = z_ref[...].astype(F32)
    t = y * (z * jax.nn.sigmoid(z))
    parts = []
    for gg in range(t.shape[1] // SSD_GROUP_W):
        tg = t[:, gg * SSD_GROUP_W:(gg + 1) * SSD_GROUP_W]
        parts.append(tg * lax.rsqrt(jnp.mean(tg * tg, axis=-1, keepdims=True) + EPS))
    tn = (jnp.concatenate(parts, axis=1) * ng_ref[...]).astype(BF16)
    contrib = jnp.dot(tn, wo_ref[...], preferred_element_type=F32)

    @pl.when(k == 0)
    def _():
        acc[...] = contrib

    @pl.when(k > 0)
    def _():
        acc[...] += contrib

    @pl.when(k == pl.num_programs(1) - 1)
    def _():
        _resid_norm_route(acc[...], x_ref, gate_ref, g_ref, sh_ref, sc_ref, rw_ref, rb_ref, xn_ref, hp_ref, lg_ref)


def _ssd_out(yd, cv, zx, skip_x, norm_g, w_o, xa, geo, mod, g2, rw, rb):
    d = xa.shape[1]
    tm = _pow2_tile(512, geo.s)
    tk = 1024
    ep_in, ep_out = _epilogue_specs(geo, tm, d, 2)
    return pl.pallas_call(
        _ssd_out_kernel,
        grid=(geo.n_lat // tm, SSD_D_INNER // tk),
        in_specs=[
            pl.BlockSpec((1, tm, tk), lambda i, k: (0, i, k)),
            pl.BlockSpec((1, tm, tk), lambda i, k: (1, i, k)),
            pl.BlockSpec((tm, tk), lambda i, k: (i, k)),
            pl.BlockSpec((tm, tk), lambda i, k: (i, k)),
            pl.BlockSpec((1, tk), lambda i, k: (0, k)),
            pl.BlockSpec((1, tk), lambda i, k: (0, k)),
            pl.BlockSpec((tk, d), lambda i, k: (k, 0)),
        ] + ep_in,
        out_specs=ep_out,
        out_shape=_epilogue_out_shapes(geo.n_lat, d),
        scratch_shapes=[pltpu.VMEM((tm, d), F32)],
        compiler_params=_params(("arbitrary", "arbitrary"), 52),
        name="ssd_out",
    )(yd, yd, cv, zx, skip_x, norm_g.astype(F32).reshape(1, SSD_D_INNER), w_o, xa, mod, g2.reshape(1, d), mod, mod,
      rw, rb)


def kernel(x, c, ctx, c_ctx, ada_w, ada_b, norm_g, attn_w_in, attn_w_o, da_q_norm_g, da_k_norm_g, da_lambda,
           da_subln_g, gm_ln_g, gm_ln_b, gm_w_s, gm_b_s, ssd_w_in, ssd_conv_w, ssd_conv_b, ssd_dt_bias, ssd_a_log,
           ssd_d_skip, ssd_norm_g, ssd_w_o, router_w, router_b, moe_w1, moe_b1, moe_w2, moe_b2):
    bsz, n, d = x.shape
    geo = _Geo(bsz, n, ctx.shape[1])
    assert ada_w.shape[0] == 2 and d == D_MODEL, "even (attention/gMLP) layer followed by a final odd (SSD) layer"

    n_rows = -(-(bsz + 1) // 8) * 8
    cond = jnp.zeros((n_rows, d), F32).at[:bsz].set(c).at[bsz].set(c_ctx)
    mods = _adaln_mods(cond, ada_w, ada_b)
    xa = jnp.concatenate([x.reshape(geo.n_lat, d), ctx.reshape(geo.n_ctx, d)], axis=0)

    mod = mods[0].reshape(n_rows, 1, 6 * d)
    proj = _norm_proj(xa, geo, norm_g[0, 0], mod, attn_w_in[0].astype(BF16), BF16, 1024)
    qk = _qk_prep(proj, geo, da_q_norm_g[0], da_k_norm_g[0])
    lam_init = 0.8 - 0.6 * math.exp(-0.3 * 0)
    lv = da_lambda[0].astype(F32)
    lam = jnp.exp(jnp.sum(lv[0] * lv[1])) - jnp.exp(jnp.sum(lv[2] * lv[3])) + lam_init
    attn = _diff_attention(qk, proj, geo, lam, da_subln_g[0], lam_init)
    gm = _gmlp(proj, geo, gm_ln_g[0], gm_ln_b[0], gm_w_s[0], gm_b_s[0])
    rw, rb = _router_operands(router_w[0], router_b[0])
    xn, hp, logits = _attn_out(attn, gm, attn_w_o[0].astype(BF16), xa, geo, mod, norm_g[0, 1], rw, rb)
    xa = _moe_layer(xn, hp, logits, geo, geo.m, mod, moe_w1, moe_b1, moe_w2, moe_b2, 0)

    mod = mods[1].reshape(n_rows, 1, 6 * d)
    w_in = ssd_w_in[0].astype(BF16)
    n_zx = SSD_D_INNER + SSD_CONV_CH
    zx = _norm_proj(xa, geo, norm_g[1, 0], mod, w_in[:, :n_zx], BF16, 1024)
    dt_raw = _norm_proj(xa, geo, norm_g[1, 0], mod, w_in[:, n_zx:], F32, SSD_DT_W)
    dt, ac = _ssd_prep(dt_raw, geo, ssd_dt_bias[0], ssd_a_log[0])
    cv = _conv_silu(zx, geo, ssd_conv_w[0], ssd_conv_b[0])
    yd = _ssd_scan(cv, dt, ac, geo)
    skip_x = jnp.repeat(ssd_d_skip[0].astype(F32).reshape(-1), SSD_HEAD_DIM).reshape(1, SSD_D_INNER)
    rw, rb = _router_operands(router_w[1], router_b[1])
    xn, hp, logits = _ssd_out(yd, cv, zx, skip_x, ssd_norm_g[0], ssd_w_o[0].astype(BF16), xa, geo, mod,
                              norm_g[1, 1], rw, rb)
    out = _moe_layer(xn, hp, logits, geo, geo.n_lat, mod, moe_w1, moe_b1, moe_w2, moe_b2, 1)
    return out.reshape(bsz, n, d)
```

```python
import functools
import math

import jax
import jax.numpy as jnp
from jax import lax
from jax.experimental import pallas as pl
from jax.experimental.pallas import tpu as pltpu

F32 = jnp.float32
BF16 = jnp.bfloat16
U32 = jnp.uint32
HIGHEST = lax.Precision.HIGHEST

D_MODEL = 2048
GRID_W = 64
EPS = 1e-6
LANES = 128
HEAD_DIM = 128
DA_HEADS = 8
DA_QK_DIM = 64
ROPE_BASE = 10000.0
ROPE_AXIS_DIM = DA_QK_DIM // 2
ROPE_FREQS = ROPE_AXIS_DIM // 2
GM_GROUPS = 8
GM_CHUNK = 128
DA_Q_W = DA_HEADS * 2 * DA_QK_DIM
DA_V_W = DA_HEADS * HEAD_DIM
GM_W = GM_GROUPS * HEAD_DIM
EVEN_IN = 2 * DA_Q_W + DA_V_W + 2 * GM_W
SSD_D_INNER = 2 * D_MODEL
SSD_HEAD_DIM = 64
SSD_HEADS = SSD_D_INNER // SSD_HEAD_DIM
SSD_GROUPS = 8
SSD_HPG = SSD_HEADS // SSD_GROUPS
SSD_GROUP_W = SSD_HPG * SSD_HEAD_DIM
SSD_STATE = 128
SSD_CONV_W = 7
SSD_CHUNK = 128
SSD_BC_W = 2 * 2 * SSD_GROUPS * SSD_STATE
SSD_CONV_CH = SSD_D_INNER + SSD_BC_W
SSD_DT_W = 2 * SSD_HEADS
N_EXPERTS = 32
TOP_K = 4
D_EXPERT = D_MODEL
SWIGLU_LIMIT = 7.0
SWIGLU_ALPHA = 1.702

MOE_TM = 2304
MOE_RB = 256
MOE_TF = 256
MOE_P1_BLOCKS = 3
SUBLANES = 8
COMBINE_TC = 512
ROUTER_PAD = 128
MIB = 2 ** 20
V7X_VMEM_BYTES = 64 * MIB


def _params(sem, vmem_mib):
    assert vmem_mib * MIB < V7X_VMEM_BYTES
    return pltpu.CompilerParams(dimension_semantics=sem, vmem_limit_bytes=vmem_mib * MIB)


def _pow2_tile(limit, *sizes):
    t = limit
    while any(s % t for s in sizes):
        t //= 2
    return t


class _Geo:
    def __init__(self, b, s, c):
        self.b, self.s, self.c = b, s, c
        self.n_lat, self.n_ctx = b * s, b * c
        self.m = self.n_lat + self.n_ctx

    def mod_row(self, i, tm):
        return jnp.where(i < self.n_lat // tm, i // (self.s // tm), self.b)


def _adaln_kernel(c_ref, w_ref, b_ref, o_ref):
    c = c_ref[...]
    s = c * jax.nn.sigmoid(c)
    o_ref[0] = jnp.dot(s.astype(BF16), w_ref[0].astype(BF16), preferred_element_type=F32) + b_ref[0]


def _adaln_mods(cond, ada_w, ada_b):
    n_l, d, n = ada_w.shape
    r = cond.shape[0]
    tn = 1024
    return pl.pallas_call(
        _adaln_kernel,
        grid=(n_l, n // tn),
        in_specs=[
            pl.BlockSpec((r, d), lambda l, j: (0, 0)),
            pl.BlockSpec((1, d, tn), lambda l, j: (l, 0, j)),
            pl.BlockSpec((1, 1, tn), lambda l, j: (l, 0, j)),
        ],
        out_specs=pl.BlockSpec((1, r, tn), lambda l, j: (l, 0, j)),
        out_shape=jax.ShapeDtypeStruct((n_l, r, n), F32),
        compiler_params=_params(("arbitrary", "arbitrary"), 40),
        name="adaln",
    )(cond, ada_w, ada_b.reshape(n_l, 1, n))


def _modulated_norm(x, g, shift, scale):
    y = x * lax.rsqrt(jnp.mean(x * x, axis=-1, keepdims=True) + EPS)
    return y * g * (1.0 + scale) + shift


def _norm_proj_kernel(x_ref, g_ref, sh_ref, sc_ref, w_ref, o_ref, h_scr):
    @pl.when(pl.program_id(1) == 0)
    def _():
        h_scr[...] = _modulated_norm(x_ref[...], g_ref[...], sh_ref[0], sc_ref[0]).astype(BF16)

    o_ref[...] = jnp.dot(h_scr[...], w_ref[...], preferred_element_type=F32).astype(o_ref.dtype)


def _norm_proj(xa, geo, g, mod, w, out_dtype, tn):
    d = xa.shape[1]
    n = w.shape[1]
    tm = _pow2_tile(1024, geo.s, geo.n_ctx)
    row = lambda i, j: (geo.mod_row(i, tm), 0, 0)
    row_scale = lambda i, j: (geo.mod_row(i, tm), 0, 1)
    return pl.pallas_call(
        _norm_proj_kernel,
        grid=(geo.m // tm, n // tn),
        in_specs=[
            pl.BlockSpec((tm, d), lambda i, j: (i, 0)),
            pl.BlockSpec((1, d), lambda i, j: (0, 0)),
            pl.BlockSpec((1, 1, d), row),
            pl.BlockSpec((1, 1, d), row_scale),
            pl.BlockSpec((d, tn), lambda i, j: (0, j)),
        ],
        out_specs=pl.BlockSpec((tm, tn), lambda i, j: (i, j)),
        out_shape=jax.ShapeDtypeStruct((geo.m, n), out_dtype),
        scratch_shapes=[pltpu.VMEM((tm, d), BF16)],
        compiler_params=_params(("arbitrary", "arbitrary"), 48),
        name="norm_proj",
    )(xa, g.reshape(1, d), mod, mod, w)


def _qk_prep_kernel(t_ref, g_ref, cos_ref, sin_ref, o_ref):
    t = t_ref[...].astype(F32)
    lane = lax.broadcasted_iota(jnp.int32, t.shape, 1)
    first = lane < DA_QK_DIM
    sq = t * t
    sa = jnp.sum(jnp.where(first, sq, 0.0), axis=-1, keepdims=True)
    sb = jnp.sum(jnp.where(first, 0.0, sq), axis=-1, keepdims=True)
    ms = jnp.where(first, sa, sb) * (1.0 / DA_QK_DIM)
    y = t * lax.rsqrt(ms + EPS) * g_ref[0]
    partner = jnp.where((lane & ROPE_FREQS) == 0,
                        pltpu.roll(y, LANES - ROPE_FREQS, 1), pltpu.roll(y, ROPE_FREQS, 1))
    o_ref[...] = (y * cos_ref[...] + partner * sin_ref[...]).astype(BF16)


def _rope_tables(geo, tm):
    pos = jnp.arange(geo.s)
    inv = ROPE_BASE ** (-jnp.arange(ROPE_FREQS, dtype=F32) * 2.0 / ROPE_AXIS_DIM)
    ang = jnp.stack([pos // GRID_W, pos % GRID_W], axis=-1).astype(F32)[..., None] * inv
    cos, sin = jnp.cos(ang), jnp.sin(ang)
    cos64 = jnp.concatenate([cos, cos], axis=-1).reshape(geo.s, DA_QK_DIM)
    sin64 = jnp.concatenate([-sin, sin], axis=-1).reshape(geo.s, DA_QK_DIM)
    cos_t = jnp.concatenate([jnp.tile(cos64, (1, 2)), jnp.ones((tm, HEAD_DIM), F32)], axis=0)
    sin_t = jnp.concatenate([jnp.tile(sin64, (1, 2)), jnp.zeros((tm, HEAD_DIM), F32)], axis=0)
    return cos_t, sin_t


def _qk_prep(proj, geo, gq, gk):
    tm = _pow2_tile(512, geo.s, geo.n_ctx)
    cos_t, sin_t = _rope_tables(geo, tm)
    scale = DA_QK_DIM ** -0.5 * math.log2(math.e)
    gains = jnp.stack([jnp.tile(gq.astype(F32) * scale, 2), jnp.tile(gk.astype(F32), 2)]).reshape(2, 1, HEAD_DIM)
    n_lat_tiles, per_seq = geo.n_lat // tm, geo.s // tm
    tab = lambda i, j: (jnp.where(i < n_lat_tiles, i % per_seq, per_seq), 0)
    n_blocks = (DA_Q_W * 2) // HEAD_DIM
    return pl.pallas_call(
        _qk_prep_kernel,
        grid=(geo.m // tm, n_blocks),
        in_specs=[
            pl.BlockSpec((tm, HEAD_DIM), lambda i, j: (i, j)),
            pl.BlockSpec((1, 1, HEAD_DIM), lambda i, j: (j // DA_HEADS, 0, 0)),
            pl.BlockSpec((tm, HEAD_DIM), tab),
            pl.BlockSpec((tm, HEAD_DIM), tab),
        ],
        out_specs=pl.BlockSpec((tm, HEAD_DIM), lambda i, j: (i, j)),
        out_shape=jax.ShapeDtypeStruct((geo.m, 2 * DA_Q_W), BF16),
        compiler_params=_params(("arbitrary", "arbitrary"), 32),
        name="qk_prep",
    )(proj, gains, cos_t, sin_t)


def _fori_unrolled(n, unroll, body, init):
    def outer(j, carry):
        for u in range(unroll):
            carry = body(j * unroll + u, carry)
        return carry
    return lax.fori_loop(0, n // unroll, outer, init)


def _fold_lanes(x, op):
    out = x[:, :LANES]
    for j in range(1, x.shape[1] // LANES):
        out = op(out, x[:, j * LANES:(j + 1) * LANES])
    return out


def _attn_kernel(lam_ref, q_ref, kl_ref, kc_ref, vl_ref, vc_ref, g_ref, o_ref, s_lat, s_ctx, *, tk, n_lat_chunks,
                 nq_lat, sub_scale):
    q = q_ref[...]
    tq = q.shape[0]
    lane = lax.broadcasted_iota(jnp.int32, q.shape, 1)
    zero = jnp.zeros_like(q)
    qs = (jnp.where(lane < DA_QK_DIM, q, zero), jnp.where(lane < DA_QK_DIM, zero, q))
    nt = (((1,), (1,)), ((), ()))
    n_chunks = jnp.where(pl.program_id(2) < nq_lat, n_lat_chunks, 0)

    def scores(c, m):
        k = kl_ref[pl.ds(pl.multiple_of(c * tk, tk), tk), :]
        new = []
        for i in range(2):
            s = lax.dot_general(qs[i], k, nt, preferred_element_type=F32)
            s_lat[i, c] = s
            new.append(jnp.maximum(m[i], _fold_lanes(s, jnp.maximum)))
        return tuple(new)

    neg = jnp.full((tq, LANES), -jnp.inf, F32)
    m_lat = _fori_unrolled(n_chunks, math.gcd(n_lat_chunks, 4), scores, (neg, neg))
    m = []
    for i in range(2):
        s = lax.dot_general(qs[i], kc_ref[...], nt, preferred_element_type=F32)
        s_ctx[i] = s
        m.append(jnp.max(jnp.maximum(m_lat[i], _fold_lanes(s, jnp.maximum)), axis=-1, keepdims=True))

    def weighted(c, carry):
        v = vl_ref[pl.ds(pl.multiple_of(c * tk, tk), tk), :]
        new = []
        for i in range(2):
            p = jnp.exp2(s_lat[i, c] - m[i])
            new.append(carry[2 * i] + _fold_lanes(p, jnp.add))
            new.append(carry[2 * i + 1] + jnp.dot(p.astype(BF16), v, preferred_element_type=F32))
        return tuple(new)

    z = jnp.zeros((tq, LANES), F32)
    part = _fori_unrolled(n_chunks, math.gcd(n_lat_chunks, 4), weighted, (z, z, z, z))
    outs = []
    for i in range(2):
        p = jnp.exp2(s_ctx[i] - m[i])
        l = jnp.sum(part[2 * i] + _fold_lanes(p, jnp.add), axis=-1, keepdims=True)
        outs.append((part[2 * i + 1] + jnp.dot(p.astype(BF16), vc_ref[...], preferred_element_type=F32)) / l)
    o = outs[0] - lam_ref[0, 0] * outs[1]
    y = o * lax.rsqrt(jnp.mean(o * o, axis=-1, keepdims=True) + EPS) * g_ref[...]
    o_ref[...] = (y * sub_scale).astype(BF16)


def _diff_attention(qk, proj, geo, lam, subln_g, lam_init):
    tq = _pow2_tile(256, geo.s, geo.c)
    tk = _pow2_tile(512, geo.s)
    nq_lat, nq_ctx = geo.s // tq, geo.c // tq
    k_col0, v_col0 = DA_Q_W // HEAD_DIM, 2 * DA_Q_W // HEAD_DIM
    ctx_blk0 = geo.n_lat // geo.c

    def q_idx(b, h, qi, col0=0):
        lat = b * nq_lat + qi
        ctx = geo.n_lat // tq + b * nq_ctx + (qi - nq_lat)
        return (jnp.where(qi < nq_lat, lat, ctx), col0 + h)

    kernel = functools.partial(_attn_kernel, tk=tk, n_lat_chunks=geo.s // tk, nq_lat=nq_lat,
                               sub_scale=1.0 - lam_init)
    return pl.pallas_call(
        kernel,
        grid=(geo.b, DA_HEADS, nq_lat + nq_ctx),
        in_specs=[
            pl.BlockSpec(memory_space=pltpu.SMEM),
            pl.BlockSpec((tq, HEAD_DIM), q_idx),
            pl.BlockSpec((geo.s, HEAD_DIM), lambda b, h, qi: (b, k_col0 + h)),
            pl.BlockSpec((geo.c, HEAD_DIM), lambda b, h, qi: (ctx_blk0 + b, k_col0 + h)),
            pl.BlockSpec((geo.s, HEAD_DIM), lambda b, h, qi: (b, v_col0 + h)),
            pl.BlockSpec((geo.c, HEAD_DIM), lambda b, h, qi: (ctx_blk0 + b, v_col0 + h)),
            pl.BlockSpec((1, HEAD_DIM), lambda b, h, qi: (0, 0)),
        ],
        out_specs=pl.BlockSpec((tq, HEAD_DIM), q_idx),
        out_shape=jax.ShapeDtypeStruct((geo.m, DA_V_W), BF16),
        scratch_shapes=[pltpu.VMEM((2, geo.s // tk, tq, tk), F32), pltpu.VMEM((2, tq, geo.c), F32)],
        compiler_params=_params(("arbitrary", "arbitrary", "arbitrary"), 40),
        name="diff_attn",
    )(lam.reshape(1, 1), qk, qk, qk, proj, proj, subln_g.reshape(1, HEAD_DIM).astype(F32))


def _gelu(x):
    return 0.5 * x * (1.0 + lax.erf(x * (1.0 / math.sqrt(2.0))))


def _gmlp_kernel(u_ref, v_ref, lng_ref, lnb_ref, ws_ref, bs_ref, o_ref):
    tm = u_ref.shape[0]
    for r in range(tm // GM_CHUNK):
        rows = slice(r * GM_CHUNK, (r + 1) * GM_CHUNK)
        for g in range(GM_GROUPS):
            cols = slice(g * HEAD_DIM, (g + 1) * HEAD_DIM)
            v = _gelu(v_ref[rows, cols].astype(F32))
            mu = jnp.mean(v, axis=-1, keepdims=True)
            var = jnp.mean(jnp.square(v - mu), axis=-1, keepdims=True)
            vn = (v - mu) * lax.rsqrt(var + EPS) * lng_ref[:, cols] + lnb_ref[:, cols]
            s = jnp.dot(ws_ref[g], vn.astype(BF16), preferred_element_type=F32) + bs_ref[:, cols]
            o_ref[rows, cols] = (_gelu(u_ref[rows, cols].astype(F32)) * s).astype(BF16)


def _gmlp(proj, geo, ln_g, ln_b, w_s, b_s):
    tm = _pow2_tile(256, geo.s, geo.c)
    u_blk = (2 * DA_Q_W + DA_V_W) // GM_W
    bs_full = jnp.repeat(b_s.astype(F32).T, HEAD_DIM, axis=1)
    return pl.pallas_call(
        _gmlp_kernel,
        grid=(geo.m // tm,),
        in_specs=[
            pl.BlockSpec((tm, GM_W), lambda i: (i, u_blk)),
            pl.BlockSpec((tm, GM_W), lambda i: (i, u_blk + 1)),
            pl.BlockSpec((1, GM_W), lambda i: (0, 0)),
            pl.BlockSpec((1, GM_W), lambda i: (0, 0)),
            pl.BlockSpec((GM_GROUPS, GM_CHUNK, GM_CHUNK), lambda i: (0, 0, 0)),
            pl.BlockSpec((GM_CHUNK, GM_W), lambda i: (0, 0)),
        ],
        out_specs=pl.BlockSpec((tm, GM_W), lambda i: (i, 0)),
        out_shape=jax.ShapeDtypeStruct((geo.m, GM_W), BF16),
        compiler_params=_params(("arbitrary",), 32),
        name="gmlp",
    )(proj, proj, ln_g.reshape(1, GM_W).astype(F32), ln_b.reshape(1, GM_W).astype(F32), w_s.astype(BF16), bs_full)


def _pack_bf16_pair(lo, hi):
    lo_bits = lax.bitcast_convert_type(lo.astype(BF16).astype(F32), U32)
    hi_bits = lax.bitcast_convert_type(hi.astype(BF16).astype(F32), U32)
    return (lo_bits >> 16) | (hi_bits & jnp.uint32(0xFFFF0000))


def _store_token_major(ref, packed):
    rows, width = packed.shape
    k = width // LANES
    for c in range(k):
        ref[pl.ds(c, rows, stride=k), :] = packed[:, c * LANES:(c + 1) * LANES]


def _load_token_major(ref, row0, rows, k, c):
    return ref[pl.ds(row0 * k + c, rows, stride=k), :]


def _resid_norm_route(y, x_ref, gate_ref, g_ref, sh_ref, sc_ref, rw_ref, rb_ref, xn_ref, hp_ref, lg_ref):
    xn = x_ref[...] + gate_ref[0] * y
    xn_ref[...] = xn
    h = _modulated_norm(xn, g_ref[...], sh_ref[0], sc_ref[0])
    lg_ref[...] = jnp.dot(h, rw_ref[...], preferred_element_type=F32, precision=HIGHEST) + rb_ref[...]
    half = h.shape[1] // 2
    _store_token_major(hp_ref, _pack_bf16_pair(h[:, :half], h[:, half:]))


def _epilogue_specs(geo, tm, d, grid_rank):
    def ix(f):
        return (lambda i: f(i)) if grid_rank == 1 else (lambda i, k: f(i))
    mod = lambda k: ix(lambda i: (geo.mod_row(i, tm), 0, k))
    in_specs = [
        pl.BlockSpec((tm, d), ix(lambda i: (i, 0))),
        pl.BlockSpec((1, 1, d), mod(2)),
        pl.BlockSpec((1, d), ix(lambda i: (0, 0))),
        pl.BlockSpec((1, 1, d), mod(3)),
        pl.BlockSpec((1, 1, d), mod(4)),
        pl.BlockSpec((d, ROUTER_PAD), ix(lambda i: (0, 0))),
        pl.BlockSpec((1, ROUTER_PAD), ix(lambda i: (0, 0))),
    ]
    out_specs = [
        pl.BlockSpec((tm, d), ix(lambda i: (i, 0))),
        pl.BlockSpec((tm * (d // 2 // LANES), LANES), ix(lambda i: (i, 0))),
        pl.BlockSpec((tm, ROUTER_PAD), ix(lambda i: (i, 0))),
    ]
    return in_specs, out_specs


def _epilogue_out_shapes(n_rows, d):
    return [jax.ShapeDtypeStruct((n_rows, d), F32), jax.ShapeDtypeStruct((n_rows * (d // 2 // LANES), LANES), U32),
            jax.ShapeDtypeStruct((n_rows, ROUTER_PAD), F32)]


def _router_operands(router_w, router_b):
    d = router_w.shape[0]
    rw = jnp.zeros((d, ROUTER_PAD), F32).at[:, :N_EXPERTS].set(router_w.astype(F32))
    rb = jnp.zeros((1, ROUTER_PAD), F32).at[0, :N_EXPERTS].set(router_b.astype(F32))
    return rw, rb


def _attn_out_kernel(a_ref, gm_ref, wo_ref, x_ref, gate_ref, g_ref, sh_ref, sc_ref, rw_ref, rb_ref,
                     xn_ref, hp_ref, lg_ref):
    y = jnp.dot(a_ref[...], wo_ref[:DA_V_W, :], preferred_element_type=F32)
    y = y + jnp.dot(gm_ref[...], wo_ref[DA_V_W:, :], preferred_element_type=F32)
    _resid_norm_route(y, x_ref, gate_ref, g_ref, sh_ref, sc_ref, rw_ref, rb_ref, xn_ref, hp_ref, lg_ref)


def _attn_out(attn, gm, w_o, xa, geo, mod, g2, rw, rb):
    d = xa.shape[1]
    tm = _pow2_tile(512, geo.s, geo.n_ctx)
    ep_in, ep_out = _epilogue_specs(geo, tm, d, 1)
    return pl.pallas_call(
        _attn_out_kernel,
        grid=(geo.m // tm,),
        in_specs=[
            pl.BlockSpec((tm, DA_V_W), lambda i: (i, 0)),
            pl.BlockSpec((tm, GM_W), lambda i: (i, 0)),
            pl.BlockSpec((DA_V_W + GM_W, d), lambda i: (0, 0)),
        ] + ep_in,
        out_specs=ep_out,
        out_shape=_epilogue_out_shapes(geo.m, d),
        compiler_params=_params(("arbitrary",), 52),
        name="attn_out",
    )(attn, gm, w_o, xa, mod, g2.reshape(1, d), mod, mod, rw, rb)


def _route(logits, n_tok):
    top_val, top_idx = lax.top_k(logits[:, :N_EXPERTS], TOP_K)
    gates = jax.nn.softmax(top_val, axis=-1)
    n_asg = n_tok * TOP_K
    n_tiles = n_asg // MOE_TM + N_EXPERTS
    flat_e = top_idx.reshape(-1).astype(jnp.int32)
    order = jnp.argsort(flat_e, stable=True).astype(jnp.int32)
    experts = jnp.arange(N_EXPERTS, dtype=jnp.int32)
    counts = jnp.sum((flat_e[:, None] == experts[None, :]).astype(jnp.int32), axis=0)
    grp_start = jnp.cumsum(counts) - counts
    tiles_e = (counts + MOE_TM - 1) // MOE_TM
    tile_end = jnp.cumsum(tiles_e)
    tile_start = tile_end - tiles_e
    n_valid = tile_end[-1]
    t = jnp.arange(n_tiles, dtype=jnp.int32)
    t_eff = jnp.minimum(t, n_valid - 1)
    tile_e = jnp.minimum(jnp.searchsorted(tile_end, t_eff, side="right"), N_EXPERTS - 1).astype(jnp.int32)
    tile_off = (t_eff - tile_start[tile_e]) * MOE_TM
    tile_rows = jnp.where(t < n_valid, jnp.clip(counts[tile_e] - tile_off, 0, MOE_TM), 0).astype(jnp.int32)
    r = jnp.arange(MOE_TM, dtype=jnp.int32)[None, :]
    order_pad = jnp.concatenate([order, jnp.zeros((MOE_TM,), jnp.int32)])
    seg = jax.vmap(lambda s0: lax.dynamic_slice(order_pad, (s0,), (MOE_TM,)))(grp_start[tile_e] + tile_off)
    asg = jnp.where(r < tile_rows[:, None], seg, 0)
    row_tok = (asg // TOP_K).reshape(n_tiles, 1, MOE_TM)
    row_slot = ((asg % TOP_K) * n_tok + asg // TOP_K).reshape(n_tiles, 1, MOE_TM)
    return tile_e, tile_rows, row_tok, row_slot, gates


def _unpack_bf16_pair(u):
    return lax.bitcast_convert_type(u << 16, F32), lax.bitcast_convert_type(u & jnp.uint32(0xFFFF0000), F32)


def _row_blocks(n_rows):
    return (n_rows + MOE_RB - 1) // MOE_RB


def _padded_blocks(n_blocks):
    return (n_blocks + MOE_P1_BLOCKS - 1) // MOE_P1_BLOCKS * MOE_P1_BLOCKS


def _moe_kernel(te_ref, tr_ref, tok_ref, tok_next_ref, slot_ref, hp_hbm, w1g_ref, w1l_ref, b1g_ref, b1l_ref,
                w2_ref, b2_ref, ys_hbm, xbuf, xs, act, w2b, ystage, gsem, ssem):
    t, s = pl.program_id(0), pl.program_id(1)
    n_tiles = pl.num_programs(0)
    rows = tr_ref[t]
    nrb = _row_blocks(rows)
    nf = act.shape[0]
    half = xs.shape[1] // 2
    kt = half // LANES
    blk_rows = MOE_RB * kt
    grp = MOE_RB // SUBLANES
    prefetched = jnp.where(t > 0, _padded_blocks(_row_blocks(tr_ref[jnp.maximum(t - 1, 0)])), 0)

    def gather_start(rows_ref, r_dyn, u):
        dst = pl.multiple_of(r_dyn * (SUBLANES * kt), SUBLANES * kt) + u * kt
        pltpu.make_async_copy(hp_hbm.at[rows_ref[0, 0, r_dyn * SUBLANES + u]], xbuf.at[pl.ds(dst, kt), :],
                              gsem.at[0]).start()

    def gather_wait(n_blocks):
        def wait_block(i, _):
            blk = xbuf.at[pl.ds(pl.multiple_of(i * blk_rows, blk_rows), blk_rows), :]
            pltpu.make_async_copy(blk, blk, gsem.at[0]).wait()
            return 0
        lax.fori_loop(0, n_blocks, wait_block, 0)

    @pl.when(s == 0)
    def _():
        def top_up(j, _):
            for u in range(SUBLANES):
                gather_start(tok_ref, j, u)
            return 0
        lax.fori_loop(prefetched * grp, nrb * grp, top_up, 0)
        gather_wait(jnp.maximum(prefetched, nrb))

        def unpack_block(i, _):
            r0 = pl.multiple_of(i * MOE_RB, MOE_RB)
            for c in range(kt):
                lo, hi = _unpack_bf16_pair(_load_token_major(xbuf, r0, MOE_RB, kt, c))
                xs[pl.ds(r0, MOE_RB), c * LANES:(c + 1) * LANES] = lo.astype(BF16)
                xs[pl.ds(r0, MOE_RB), half + c * LANES:half + (c + 1) * LANES] = hi.astype(BF16)
            return 0
        lax.fori_loop(0, nrb, unpack_block, 0)

        def zero_block(i, _):
            xs[pl.ds(pl.multiple_of(i * MOE_RB, MOE_RB), MOE_RB), :] = jnp.zeros((MOE_RB, xs.shape[1]), BF16)
            return 0
        lax.fori_loop(nrb, _padded_blocks(nrb), zero_block, 0)

    @pl.when(jnp.logical_and(s < nf, nrb > 0))
    def _():
        w1g = w1g_ref[0, 0].astype(BF16)
        w1l = w1l_ref[0, 0].astype(BF16)
        w2b[pl.ds(pl.multiple_of(s * MOE_TF, MOE_TF), MOE_TF), :] = w2_ref[0, 0].astype(BF16)

        n_padded = _padded_blocks(nrb)
        share = grp // nf

        def iteration(it, _):
            for b in range(MOE_P1_BLOCKS):
                i = it * MOE_P1_BLOCKS + b
                r0 = pl.multiple_of(i * MOE_RB, MOE_RB)
                g0 = (s * n_padded + i) * share
                for j in range(share):
                    for u in range(SUBLANES):
                        gather_start(tok_next_ref, g0 + j, u)
                x = xs[pl.ds(r0, MOE_RB), :]
                g = jnp.minimum(jnp.dot(x, w1g, preferred_element_type=F32) + b1g_ref[0, 0], SWIGLU_LIMIT)
                lin = jnp.clip(jnp.dot(x, w1l, preferred_element_type=F32) + b1l_ref[0, 0],
                               -SWIGLU_LIMIT, SWIGLU_LIMIT)
                act[s, pl.ds(r0, MOE_RB), :] = (g * jax.nn.sigmoid(SWIGLU_ALPHA * g) * (lin + 1.0)).astype(BF16)
            return 0
        lax.fori_loop(0, n_padded // MOE_P1_BLOCKS, iteration, 0)

    def scatter_start(stage_slot, r0, g, u):
        src = g * (SUBLANES * kt) + u * kt
        if not isinstance(g, int):
            src = pl.multiple_of(g * (SUBLANES * kt), SUBLANES * kt) + u * kt
        pltpu.make_async_copy(ystage.at[stage_slot, pl.ds(src, kt), :],
                              ys_hbm.at[slot_ref[0, 0, r0 + g * SUBLANES + u]], ssem.at[stage_slot]).start()

    def scatter_wait(stage_slot, n_rows):
        blk = ystage.at[stage_slot, pl.ds(0, n_rows * kt), :]
        pltpu.make_async_copy(blk, blk, ssem.at[stage_slot]).wait()

    @pl.when(jnp.logical_and(s == nf, nrb > 0))
    def _():
        def staged(i):
            r0 = pl.multiple_of(i * MOE_RB, MOE_RB)
            a = jnp.concatenate([act[f, pl.ds(r0, MOE_RB), :] for f in range(nf)], axis=1)
            y = jnp.dot(a, w2b[...], preferred_element_type=F32) + b2_ref[0, 0]
            return _pack_bf16_pair(y[:, :half], y[:, half:])

        _store_token_major(ystage.at[0], staged(0))

        def block(i, _):
            stage_slot = i % 2
            r_prev = pl.multiple_of((i - 1) * MOE_RB, MOE_RB)
            for g in range(grp):
                for u in range(SUBLANES):
                    scatter_start(1 - stage_slot, r_prev, g, u)
            packed = staged(i)

            @pl.when(i >= 2)
            def _():
                scatter_wait(stage_slot, MOE_RB)

            _store_token_major(ystage.at[stage_slot], packed)
            return 0
        lax.fori_loop(1, nrb, block, 0)

        last = nrb - 1
        last_slot = last % 2
        r_last = pl.multiple_of(last * MOE_RB, MOE_RB)
        n_last = rows - r_last

        def issue_group(g, _):
            for u in range(SUBLANES):
                scatter_start(last_slot, r_last, g, u)
            return 0
        lax.fori_loop(0, n_last // SUBLANES, issue_group, 0)
        for u in range(SUBLANES - 1):
            @pl.when(u < n_last % SUBLANES)
            def _():
                scatter_start(last_slot, r_last, n_last // SUBLANES, u)

        @pl.when(nrb >= 2)
        def _():
            scatter_wait(1 - last_slot, MOE_RB)

        def wait_one(r, _):
            scatter_wait(last_slot, 1)
            return 0
        lax.fori_loop(0, n_last, wait_one, 0)

    @pl.when(jnp.logical_and(t == n_tiles - 1, s == nf))
    def _():
        gather_wait(_padded_blocks(nrb))


def _moe_experts(hp, n_tok, tile_e, tile_rows, row_tok, row_slot, w1, b1, w2, b2, li):
    n_tiles = row_tok.shape[0]
    d = w2.shape[-1]
    nf = D_EXPERT // MOE_TF
    kt = d // 2 // LANES
    fz = lambda s, tr, t: jnp.where(tr[t] > 0, jnp.minimum(s, nf - 1), nf - 1)
    b1r = b1.reshape(b1.shape[0], N_EXPERTS, 1, 2 * D_EXPERT)
    b2r = b2.reshape(b2.shape[0], N_EXPERTS, 1, d)
    rows_spec = lambda f: pl.BlockSpec((1, 1, MOE_TM), f, memory_space=pltpu.SMEM)
    grid_spec = pltpu.PrefetchScalarGridSpec(
        num_scalar_prefetch=2,
        grid=(n_tiles, nf + 1),
        in_specs=[
            rows_spec(lambda t, s, te, tr: (t, 0, 0)),
            rows_spec(lambda t, s, te, tr: (jnp.minimum(t + 1, n_tiles - 1), 0, 0)),
            rows_spec(lambda t, s, te, tr: (t, 0, 0)),
            pl.BlockSpec(memory_space=pl.ANY),
            pl.BlockSpec((1, 1, d, MOE_TF), lambda t, s, te, tr: (li, te[t], 0, fz(s, tr, t))),
            pl.BlockSpec((1, 1, d, MOE_TF), lambda t, s, te, tr: (li, te[t], 0, nf + fz(s, tr, t))),
            pl.BlockSpec((1, 1, 1, MOE_TF), lambda t, s, te, tr: (li, te[t], 0, fz(s, tr, t))),
            pl.BlockSpec((1, 1, 1, MOE_TF), lambda t, s, te, tr: (li, te[t], 0, nf + fz(s, tr, t))),
            pl.BlockSpec((1, 1, MOE_TF, d), lambda t, s, te, tr: (li, te[t], fz(s, tr, t), 0)),
            pl.BlockSpec((1, 1, 1, d), lambda t, s, te, tr: (li, te[t], 0, 0)),
        ],
        out_specs=pl.BlockSpec(memory_space=pl.ANY),
        scratch_shapes=[
            pltpu.VMEM((MOE_TM * kt, LANES), U32),
            pltpu.VMEM((MOE_TM, d), BF16),
            pltpu.VMEM((nf, MOE_TM, MOE_TF), BF16),
            pltpu.VMEM((D_EXPERT, d), BF16),
            pltpu.VMEM((2, MOE_RB * kt, LANES), U32),
            pltpu.SemaphoreType.DMA((1,)),
            pltpu.SemaphoreType.DMA((2,)),
        ],
    )
    return pl.pallas_call(
        _moe_kernel,
        grid_spec=grid_spec,
        out_shape=jax.ShapeDtypeStruct((TOP_K * n_tok, kt, LANES), U32),
        compiler_params=_params(("arbitrary", "arbitrary"), 60),
        name="moe_experts",
    )(tile_e, tile_rows, row_tok, row_tok, row_slot, hp.reshape(-1, kt, LANES), w1, w1, b1r, b1r, w2, b2r)


def _combine_kernel(ys_ref, gate_ref, x_ref, g5_ref, o_ref):
    tc, d = x_ref.shape
    half = d // 2
    kt = half // LANES
    gate = gate_ref[...]
    gates = [jnp.broadcast_to(gate[:, k:k + 1], (tc, LANES)) for k in range(TOP_K)]
    for c in range(kt):
        acc_lo = acc_hi = None
        for k in range(TOP_K):
            lo, hi = _unpack_bf16_pair(_load_token_major(ys_ref.at[k], 0, tc, kt, c))
            acc_lo = gates[k] * lo if k == 0 else acc_lo + gates[k] * lo
            acc_hi = gates[k] * hi if k == 0 else acc_hi + gates[k] * hi
        lo_cols = slice(c * LANES, (c + 1) * LANES)
        hi_cols = slice(half + c * LANES, half + (c + 1) * LANES)
        o_ref[:, lo_cols] = x_ref[:, lo_cols] + g5_ref[0, :, lo_cols] * acc_lo
        o_ref[:, hi_cols] = x_ref[:, hi_cols] + g5_ref[0, :, hi_cols] * acc_hi


def _moe_combine(ys, gates, xn, geo, n_tok, mod):
    d = xn.shape[1]
    kt = d // 2 // LANES
    tc = _pow2_tile(COMBINE_TC, geo.s, geo.n_ctx)
    return pl.pallas_call(
        _combine_kernel,
        grid=(n_tok // tc,),
        in_specs=[
            pl.BlockSpec((TOP_K, tc * kt, LANES), lambda i: (0, i, 0)),
            pl.BlockSpec((tc, TOP_K), lambda i: (i, 0)),
            pl.BlockSpec((tc, d), lambda i: (i, 0)),
            pl.BlockSpec((1, 1, d), lambda i: (geo.mod_row(i, tc), 0, 5)),
        ],
        out_specs=pl.BlockSpec((tc, d), lambda i: (i, 0)),
        out_shape=jax.ShapeDtypeStruct((n_tok, d), F32),
        compiler_params=_params(("arbitrary",), 40),
        name="moe_combine",
    )(ys.reshape(TOP_K, n_tok * kt, LANES), gates, xn, mod)


def _moe_layer(xn, hp, logits, geo, n_tok, mod, w1, b1, w2, b2, li):
    tile_e, tile_rows, row_tok, row_slot, gates = _route(logits[:n_tok], n_tok)
    ys = _moe_experts(hp, n_tok, tile_e, tile_rows, row_tok, row_slot, w1, b1, w2, b2, li)
    return _moe_combine(ys, gates, xn, geo, n_tok, mod)


def _ssd_prep_kernel(raw_ref, bias_ref, aneg_ref, dt_ref, ac_ref):
    tm = raw_ref.shape[0]
    li = lax.broadcasted_iota(jnp.int32, (SSD_CHUNK, SSD_CHUNK), 0)
    si = lax.broadcasted_iota(jnp.int32, (SSD_CHUNK, SSD_CHUNK), 1)
    tri_f = (si <= li).astype(F32)
    tri_b = (si >= li).astype(F32)
    lane = lax.broadcasted_iota(jnp.int32, (SSD_CHUNK, SSD_DT_W), 1)
    for c in range(tm // SSD_CHUNK):
        rows = slice(c * SSD_CHUNK, (c + 1) * SSD_CHUNK)
        v = raw_ref[rows, :] + bias_ref[...]
        dt = jnp.maximum(v, 0.0) + jnp.log1p(jnp.exp(-jnp.abs(v)))
        a = dt * aneg_ref[...]
        fwd = jnp.dot(tri_f, a, preferred_element_type=F32, precision=HIGHEST)
        bwd = jnp.dot(tri_b, a, preferred_element_type=F32, precision=HIGHEST)
        dt_ref[rows, :] = dt
        ac_ref[rows, :] = jnp.where(lane < SSD_HEADS, fwd, bwd)


def _ssd_prep(dt_raw, geo, dt_bias, a_log):
    tm = _pow2_tile(512, geo.s, geo.c)
    a_neg = -jnp.exp(a_log.astype(F32)).reshape(1, SSD_DT_W)
    spec = pl.BlockSpec((tm, SSD_DT_W), lambda i: (i, 0))
    vec = pl.BlockSpec((1, SSD_DT_W), lambda i: (0, 0))
    return pl.pallas_call(
        _ssd_prep_kernel,
        grid=(geo.m // tm,),
        in_specs=[spec, vec, vec],
        out_specs=[spec, spec],
        out_shape=[jax.ShapeDtypeStruct((geo.m, SSD_DT_W), F32)] * 2,
        compiler_params=_params(("arbitrary",), 32),
        name="ssd_prep",
    )(dt_raw, dt_bias.astype(F32).reshape(1, SSD_DT_W), a_neg)


CONV_HALO = 16


def _conv_kernel(prev_ref, cur_ref, next_ref, w_ref, b_ref, o_ref, ext, *, geo, tm):
    i = pl.program_id(0)
    n_lat_tiles = geo.n_lat // tm
    per_lat, per_ctx = geo.s // tm, geo.c // tm
    lat = i < n_lat_tiles
    p = jnp.where(lat, i % per_lat, (i - n_lat_tiles) % per_ctx)
    first = p == 0
    last = jnp.where(lat, p == per_lat - 1, p == per_ctx - 1)
    ext[0:CONV_HALO, :] = jnp.where(first, 0.0, prev_ref[...].astype(F32))
    ext[CONV_HALO:CONV_HALO + tm, :] = cur_ref[...].astype(F32)
    ext[CONV_HALO + tm:, :] = jnp.where(last, 0.0, next_ref[...].astype(F32))
    pad = SSD_CONV_W // 2
    acc = w_ref[0:1, :] * ext[pl.ds(CONV_HALO - pad, tm), :]
    for k in range(1, SSD_CONV_W):
        acc = acc + w_ref[k:k + 1, :] * ext[pl.ds(CONV_HALO - pad + k, tm), :]
    acc = acc + b_ref[...]
    o_ref[...] = (acc * jax.nn.sigmoid(acc)).astype(BF16)


def _conv_silu(zx, geo, conv_w, conv_b):
    tm = _pow2_tile(256, geo.s, geo.c)
    tc = 1024
    col0 = SSD_D_INNER // tc
    hb = tm // CONV_HALO
    n_halo_blocks = geo.m // CONV_HALO
    return pl.pallas_call(
        functools.partial(_conv_kernel, geo=geo, tm=tm),
        grid=(geo.m // tm, SSD_CONV_CH // tc),
        in_specs=[
            pl.BlockSpec((CONV_HALO, tc), lambda i, j: (jnp.maximum(i * hb - 1, 0), col0 + j)),
            pl.BlockSpec((tm, tc), lambda i, j: (i, col0 + j)),
            pl.BlockSpec((CONV_HALO, tc), lambda i, j: (jnp.minimum((i + 1) * hb, n_halo_blocks - 1), col0 + j)),
            pl.BlockSpec((SSD_CONV_W, tc), lambda i, j: (0, j)),
            pl.BlockSpec((1, tc), lambda i, j: (0, j)),
        ],
        out_specs=pl.BlockSpec((tm, tc), lambda i, j: (i, j)),
        out_shape=jax.ShapeDtypeStruct((geo.m, SSD_CONV_CH), BF16),
        scratch_shapes=[pltpu.VMEM((tm + 2 * CONV_HALO, tc), F32)],
        compiler_params=_params(("arbitrary", "arbitrary"), 32),
        name="ssd_conv",
    )(zx, zx, zx, conv_w.astype(F32), conv_b.astype(F32).reshape(1, SSD_CONV_CH))


def _ssd_scan_kernel(xs_ref, b_ref, c_ref, dt_ref, ac_ref, y_ref, state, xg, bt_g, c_g, dt_g, ac_g, ac_t, y_g):
    d, s = pl.program_id(1), pl.program_id(2)
    q = SSD_CHUNK
    is_fwd = d == 0

    @pl.when(s == 0)
    def _():
        state[...] = jnp.zeros(state.shape, F32)

    dt2, ac2 = dt_ref[...], ac_ref[...]
    dtc = jnp.where(is_fwd, dt2[:, :SSD_HEADS], dt2[:, SSD_HEADS:])
    acc = jnp.where(is_fwd, ac2[:, :SSD_HEADS], ac2[:, SSD_HEADS:])
    ac_t[...] = ac2.T
    b_f32 = b_ref[...].astype(F32)
    for g in range(SSD_GROUPS):
        xg[g] = xs_ref[:, g * SSD_GROUP_W:(g + 1) * SSD_GROUP_W]
        bt_g[g] = b_f32[:, g * SSD_STATE:(g + 1) * SSD_STATE].T.astype(BF16)
        c_g[g] = c_ref[:, g * SSD_STATE:(g + 1) * SSD_STATE]
        dt_g[g] = dtc[:, g * SSD_HPG:(g + 1) * SSD_HPG]
        ac_g[g] = acc[:, g * SSD_HPG:(g + 1) * SSD_HPG]

    li = lax.broadcasted_iota(jnp.int32, (q, q), 0)
    si = lax.broadcasted_iota(jnp.int32, (q, q), 1)
    mask = jnp.where(is_fwd, li - si, si - li) >= 0
    lane = lax.broadcasted_iota(jnp.int32, (q, LANES), 1)
    lo_half = lane < SSD_HEAD_DIM
    lo_row = lo_half[0:1, :]

    def group(g, _):
        cg = c_g[g]
        cb = jnp.dot(cg, bt_g[g], preferred_element_type=F32)
        dtg, acg = dt_g[g], ac_g[g]
        xf = xg[g].astype(F32)
        a_cols, d_cols, decays = [], [], []
        for j in range(SSD_HPG):
            a_col = jnp.broadcast_to(acg[:, j:j + 1], (q, q))
            a_row = ac_t[pl.ds(d * SSD_HEADS + g * SSD_HPG + j, 1), :]
            seg = a_col - a_row
            decays.append((cb * jnp.exp(jnp.where(mask, seg, -jnp.inf))).astype(BF16))
            a_cols.append(a_col)
            d_cols.append(jnp.broadcast_to(dtg[:, j:j + 1], (q, q)))
        for p in range(SSD_HPG // 2):
            cols = slice(p * LANES, (p + 1) * LANES)
            acx = jnp.where(lo_half, a_cols[2 * p], a_cols[2 * p + 1])
            dtx = jnp.where(lo_half, d_cols[2 * p], d_cols[2 * p + 1])
            a_last = jnp.where(is_fwd, acx[q - 1:q, :], acx[0:1, :])
            xdt = xf[:, cols] * dtx
            xdt_b = xdt.astype(BF16)
            zero = jnp.zeros_like(xdt_b)
            rhs = jnp.concatenate([jnp.where(lo_half, xdt_b, zero), jnp.where(lo_half, zero, xdt_b)], axis=0)
            lhs = jnp.concatenate([decays[2 * p], decays[2 * p + 1]], axis=1)
            st = state[g, :, cols]
            y = jnp.dot(lhs, rhs, preferred_element_type=F32)
            y = y + jnp.dot(cg, st.astype(BF16), preferred_element_type=F32) * jnp.exp(acx)
            y_g[g, :, cols] = y
            wx = (xdt * jnp.exp(a_last - acx)).astype(BF16)
            state[g, :, cols] = st * jnp.exp(a_last) + jnp.dot(bt_g[g], wx, preferred_element_type=F32)
        return 0

    lax.fori_loop(0, SSD_GROUPS, group, 0)
    for g in range(SSD_GROUPS):
        y_ref[0, :, g * SSD_GROUP_W:(g + 1) * SSD_GROUP_W] = y_g[g].astype(BF16)


def _ssd_scan(cv, dt, ac, geo):
    q = SSD_CHUNK
    ncc, nlc = geo.c // q, geo.s // q
    bc_col0 = SSD_D_INNER // (SSD_GROUPS * SSD_STATE)

    def lat_chunk(d, s):
        c = jnp.maximum(s - ncc, 0)
        return jnp.where(d == 0, c, nlc - 1 - c)

    def row_blk(b, d, s):
        ctx_c = jnp.where(d == 0, s, ncc - 1 - s)
        return jnp.where(s < ncc, geo.b * nlc + b * ncc + ctx_c, b * nlc + lat_chunk(d, s))

    return pl.pallas_call(
        _ssd_scan_kernel,
        grid=(geo.b, 2, ncc + nlc),
        in_specs=[
            pl.BlockSpec((q, SSD_D_INNER), lambda b, d, s: (row_blk(b, d, s), 0)),
            pl.BlockSpec((q, SSD_GROUPS * SSD_STATE), lambda b, d, s: (row_blk(b, d, s), bc_col0 + 2 * d)),
            pl.BlockSpec((q, SSD_GROUPS * SSD_STATE), lambda b, d, s: (row_blk(b, d, s), bc_col0 + 2 * d + 1)),
            pl.BlockSpec((q, SSD_DT_W), lambda b, d, s: (row_blk(b, d, s), 0)),
            pl.BlockSpec((q, SSD_DT_W), lambda b, d, s: (row_blk(b, d, s), 0)),
        ],
        out_specs=pl.BlockSpec((1, q, SSD_D_INNER), lambda b, d, s: (d, b * nlc + lat_chunk(d, s), 0)),
        out_shape=jax.ShapeDtypeStruct((2, geo.n_lat, SSD_D_INNER), BF16),
        scratch_shapes=[
            pltpu.VMEM((SSD_GROUPS, SSD_STATE, SSD_GROUP_W), F32),
            pltpu.VMEM((SSD_GROUPS, q, SSD_GROUP_W), BF16),
            pltpu.VMEM((SSD_GROUPS, SSD_STATE, q), BF16),
            pltpu.VMEM((SSD_GROUPS, q, SSD_STATE), BF16),
            pltpu.VMEM((SSD_GROUPS, q, SSD_HPG), F32),
            pltpu.VMEM((SSD_GROUPS, q, SSD_HPG), F32),
            pltpu.VMEM((SSD_DT_W, q), F32),
            pltpu.VMEM((SSD_GROUPS, q, SSD_GROUP_W), F32),
        ],
        compiler_params=_params(("arbitrary", "arbitrary", "arbitrary"), 40),
        name="ssd_scan",
    )(cv, cv, cv, dt, ac)


def _ssd_out_kernel(yf_ref, yb_ref, xs_ref, z_ref, skip_ref, ng_ref, wo_ref, x_ref, gate_ref, g_ref, sh_ref,
                    sc_ref, rw_ref, rb_ref, xn_ref, hp_ref, lg_ref, acc):
    k = pl.program_id(1)
    y = yf_ref[0].astype(F32) + yb_ref[0].astype(F32) + skip_ref[...] * xs_ref[...].astype(F32)
    z = z_ref[...].astype(F32)
    t = y * (z * jax.nn.sigmoid(z))
    parts = []
    for gg in range(t.shape[1] // SSD_GROUP_W):
        tg = t[:, gg * SSD_GROUP_W:(gg + 1) * SSD_GROUP_W]
        parts.append(tg * lax.rsqrt(jnp.mean(tg * tg, axis=-1, keepdims=True) + EPS))
    tn = (jnp.concatenate(parts, axis=1) * ng_ref[...]).astype(BF16)
    contrib = jnp.dot(tn, wo_ref[...], preferred_element_type=F32)

    @pl.when(k == 0)
    def _():
        acc[...] = contrib

    @pl.when(k > 0)
    def _():
        acc[...] += contrib

    @pl.when(k == pl.num_programs(1) - 1)
    def _():
        _resid_norm_route(acc[...], x_ref, gate_ref, g_ref, sh_ref, sc_ref, rw_ref, rb_ref, xn_ref, hp_ref, lg_ref)


def _ssd_out(yd, cv, zx, skip_x, norm_g, w_o, xa, geo, mod, g2, rw, rb):
    d = xa.shape[1]
    tm = _pow2_tile(512, geo.s)
    tk = 1024
    ep_in, ep_out = _epilogue_specs(geo, tm, d, 2)
    return pl.pallas_call(
        _ssd_out_kernel,
        grid=(geo.n_lat // tm, SSD_D_INNER // tk),
        in_specs=[
            pl.BlockSpec((1, tm, tk), lambda i, k: (0, i, k)),
            pl.BlockSpec((1, tm, tk), lambda i, k: (1, i, k)),
            pl.BlockSpec((tm, tk), lambda i, k: (i, k)),
            pl.BlockSpec((tm, tk), lambda i, k: (i, k)),
            pl.BlockSpec((1, tk), lambda i, k: (0, k)),
            pl.BlockSpec((1, tk), lambda i, k: (0, k)),
            pl.BlockSpec((tk, d), lambda i, k: (k, 0)),
        ] + ep_in,
        out_specs=ep_out,
        out_shape=_epilogue_out_shapes(geo.n_lat, d),
        scratch_shapes=[pltpu.VMEM((tm, d), F32)],
        compiler_params=_params(("arbitrary", "arbitrary"), 52),
        name="ssd_out",
    )(yd, yd, cv, zx, skip_x, norm_g.astype(F32).reshape(1, SSD_D_INNER), w_o, xa, mod, g2.reshape(1, d), mod, mod,
      rw, rb)


def kernel(x, c, ctx, c_ctx, ada_w, ada_b, norm_g, attn_w_in, attn_w_o, da_q_norm_g, da_k_norm_g, da_lambda,
           da_subln_g, gm_ln_g, gm_ln_b, gm_w_s, gm_b_s, ssd_w_in, ssd_conv_w, ssd_conv_b, ssd_dt_bias, ssd_a_log,
           ssd_d_skip, ssd_norm_g, ssd_w_o, router_w, router_b, moe_w1, moe_b1, moe_w2, moe_b2):
    bsz, n, d = x.shape
    geo = _Geo(bsz, n, ctx.shape[1])
    assert ada_w.shape[0] == 2 and d == D_MODEL, "even (attention/gMLP) layer followed by a final odd (SSD) layer"

    n_rows = -(-(bsz + 1) // 8) * 8
    cond = jnp.zeros((n_rows, d), F32).at[:bsz].set(c).at[bsz].set(c_ctx)
    mods = _adaln_mods(cond, ada_w, ada_b)
    xa = jnp.concatenate([x.reshape(geo.n_lat, d), ctx.reshape(geo.n_ctx, d)], axis=0)

    mod = mods[0].reshape(n_rows, 1, 6 * d)
    proj = _norm_proj(xa, geo, norm_g[0, 0], mod, attn_w_in[0].astype(BF16), BF16, 1024)
    qk = _qk_prep(proj, geo, da_q_norm_g[0], da_k_norm_g[0])
    lam_init = 0.8 - 0.6 * math.exp(-0.3 * 0)
    lv = da_lambda[0].astype(F32)
    lam = jnp.exp(jnp.sum(lv[0] * lv[1])) - jnp.exp(jnp.sum(lv[2] * lv[3])) + lam_init
    attn = _diff_attention(qk, proj, geo, lam, da_subln_g[0], lam_init)
    gm = _gmlp(proj, geo, gm_ln_g[0], gm_ln_b[0], gm_w_s[0], gm_b_s[0])
    rw, rb = _router_operands(router_w[0], router_b[0])
    xn, hp, logits = _attn_out(attn, gm, attn_w_o[0].astype(BF16), xa, geo, mod, norm_g[0, 1], rw, rb)
    xa = _moe_layer(xn, hp, logits, geo, geo.m, mod, moe_w1, moe_b1, moe_w2, moe_b2, 0)

    mod = mods[1].reshape(n_rows, 1, 6 * d)
    w_in = ssd_w_in[0].astype(BF16)
    n_zx = SSD_D_INNER + SSD_CONV_CH
    zx = _norm_proj(xa, geo, norm_g[1, 0], mod, w_in[:, :n_zx], BF16, 1024)
    dt_raw = _norm_proj(xa, geo, norm_g[1, 0], mod, w_in[:, n_zx:], F32, SSD_DT_W)
    dt, ac = _ssd_prep(dt_raw, geo, ssd_dt_bias[0], ssd_a_log[0])
    cv = _conv_silu(zx, geo, ssd_conv_w[0], ssd_conv_b[0])
    yd = _ssd_scan(cv, dt, ac, geo)
    skip_x = jnp.repeat(ssd_d_skip[0].astype(F32).reshape(-1), SSD_HEAD_DIM).reshape(1, SSD_D_INNER)
    rw, rb = _router_operands(router_w[1], router_b[1])
    xn, hp, logits = _ssd_out(yd, cv, zx, skip_x, ssd_norm_g[0], ssd_w_o[0].astype(BF16), xa, geo, mod,
                              norm_g[1, 1], rw, rb)
    out = _moe_layer(xn, hp, logits, geo, geo.n_lat, mod, moe_w1, moe_b1, moe_w2, moe_b2, 1)
    return out.reshape(bsz, n, d)
```

```python
import functools
import math

import jax
import jax.numpy as jnp
from jax import lax
from jax.experimental import pallas as pl
from jax.experimental.pallas import tpu as pltpu

F32 = jnp.float32
BF16 = jnp.bfloat16
U32 = jnp.uint32
HIGHEST = lax.Precision.HIGHEST

D_MODEL = 2048
GRID_W = 64
EPS = 1e-6
LANES = 128
HEAD_DIM = 128
DA_HEADS = 8
DA_QK_DIM = 64
ROPE_BASE = 10000.0
ROPE_AXIS_DIM = DA_QK_DIM // 2
ROPE_FREQS = ROPE_AXIS_DIM // 2
GM_GROUPS = 8
GM_CHUNK = 128
DA_Q_W = DA_HEADS * 2 * DA_QK_DIM
DA_V_W = DA_HEADS * HEAD_DIM
GM_W = GM_GROUPS * HEAD_DIM
EVEN_IN = 2 * DA_Q_W + DA_V_W + 2 * GM_W
SSD_D_INNER = 2 * D_MODEL
SSD_HEAD_DIM = 64
SSD_HEADS = SSD_D_INNER // SSD_HEAD_DIM
SSD_GROUPS = 8
SSD_HPG = SSD_HEADS // SSD_GROUPS
SSD_GROUP_W = SSD_HPG * SSD_HEAD_DIM
SSD_STATE = 128
SSD_CONV_W = 7
SSD_CHUNK = 128
SSD_BC_W = 2 * 2 * SSD_GROUPS * SSD_STATE
SSD_CONV_CH = SSD_D_INNER + SSD_BC_W
SSD_DT_W = 2 * SSD_HEADS
N_EXPERTS = 32
TOP_K = 4
D_EXPERT = D_MODEL
SWIGLU_LIMIT = 7.0
SWIGLU_ALPHA = 1.702

MOE_TM = 2304
MOE_RB = 256
MOE_TF = 256
MOE_P1_BLOCKS = 3
SUBLANES = 8
COMBINE_TC = 512
ROUTER_PAD = 128
MIB = 2 ** 20
V7X_VMEM_BYTES = 64 * MIB


def _params(sem, vmem_mib):
    assert vmem_mib * MIB < V7X_VMEM_BYTES
    return pltpu.CompilerParams(dimension_semantics=sem, vmem_limit_bytes=vmem_mib * MIB)


def _pow2_tile(limit, *sizes):
    t = limit
    while any(s % t for s in sizes):
        t //= 2
    return t


class _Geo:
    def __init__(self, b, s, c):
        self.b, self.s, self.c = b, s, c
        self.n_lat, self.n_ctx = b * s, b * c
        self.m = self.n_lat + self.n_ctx

    def mod_row(self, i, tm):
        return jnp.where(i < self.n_lat // tm, i // (self.s // tm), self.b)


def _adaln_kernel(c_ref, w_ref, b_ref, o_ref):
    c = c_ref[...]
    s = c * jax.nn.sigmoid(c)
    o_ref[0] = jnp.dot(s.astype(BF16), w_ref[0].astype(BF16), preferred_element_type=F32) + b_ref[0]


def _adaln_mods(cond, ada_w, ada_b):
    n_l, d, n = ada_w.shape
    r = cond.shape[0]
    tn = 1024
    return pl.pallas_call(
        _adaln_kernel,
        grid=(n_l, n // tn),
        in_specs=[
            pl.BlockSpec((r, d), lambda l, j: (0, 0)),
            pl.BlockSpec((1, d, tn), lambda l, j: (l, 0, j)),
            pl.BlockSpec((1, 1, tn), lambda l, j: (l, 0, j)),
        ],
        out_specs=pl.BlockSpec((1, r, tn), lambda l, j: (l, 0, j)),
        out_shape=jax.ShapeDtypeStruct((n_l, r, n), F32),
        compiler_params=_params(("arbitrary", "arbitrary"), 40),
        name="adaln",
    )(cond, ada_w, ada_b.reshape(n_l, 1, n))


def _modulated_norm(x, g, shift, scale):
    y = x * lax.rsqrt(jnp.mean(x * x, axis=-1, keepdims=True) + EPS)
    return y * g * (1.0 + scale) + shift


def _norm_proj_kernel(x_ref, g_ref, sh_ref, sc_ref, w_ref, o_ref, h_scr):
    @pl.when(pl.program_id(1) == 0)
    def _():
        h_scr[...] = _modulated_norm(x_ref[...], g_ref[...], sh_ref[0], sc_ref[0]).astype(BF16)

    o_ref[...] = jnp.dot(h_scr[...], w_ref[...], preferred_element_type=F32).astype(o_ref.dtype)


def _norm_proj(xa, geo, g, mod, w, out_dtype, tn):
    d = xa.shape[1]
    n = w.shape[1]
    tm = _pow2_tile(1024, geo.s, geo.n_ctx)
    row = lambda i, j: (geo.mod_row(i, tm), 0, 0)
    row_scale = lambda i, j: (geo.mod_row(i, tm), 0, 1)
    return pl.pallas_call(
        _norm_proj_kernel,
        grid=(geo.m // tm, n // tn),
        in_specs=[
            pl.BlockSpec((tm, d), lambda i, j: (i, 0)),
            pl.BlockSpec((1, d), lambda i, j: (0, 0)),
            pl.BlockSpec((1, 1, d), row),
            pl.BlockSpec((1, 1, d), row_scale),
            pl.BlockSpec((d, tn), lambda i, j: (0, j)),
        ],
        out_specs=pl.BlockSpec((tm, tn), lambda i, j: (i, j)),
        out_shape=jax.ShapeDtypeStruct((geo.m, n), out_dtype),
        scratch_shapes=[pltpu.VMEM((tm, d), BF16)],
        compiler_params=_params(("arbitrary", "arbitrary"), 48),
        name="norm_proj",
    )(xa, g.reshape(1, d), mod, mod, w)


def _qk_prep_kernel(t_ref, g_ref, cos_ref, sin_ref, o_ref):
    t = t_ref[...].astype(F32)
    lane = lax.broadcasted_iota(jnp.int32, t.shape, 1)
    first = lane < DA_QK_DIM
    sq = t * t
    sa = jnp.sum(jnp.where(first, sq, 0.0), axis=-1, keepdims=True)
    sb = jnp.sum(jnp.where(first, 0.0, sq), axis=-1, keepdims=True)
    ms = jnp.where(first, sa, sb) * (1.0 / DA_QK_DIM)
    y = t * lax.rsqrt(ms + EPS) * g_ref[0]
    partner = jnp.where((lane & ROPE_FREQS) == 0,
                        pltpu.roll(y, LANES - ROPE_FREQS, 1), pltpu.roll(y, ROPE_FREQS, 1))
    o_ref[...] = (y * cos_ref[...] + partner * sin_ref[...]).astype(BF16)


def _rope_tables(geo, tm):
    pos = jnp.arange(geo.s)
    inv = ROPE_BASE ** (-jnp.arange(ROPE_FREQS, dtype=F32) * 2.0 / ROPE_AXIS_DIM)
    ang = jnp.stack([pos // GRID_W, pos % GRID_W], axis=-1).astype(F32)[..., None] * inv
    cos, sin = jnp.cos(ang), jnp.sin(ang)
    cos64 = jnp.concatenate([cos, cos], axis=-1).reshape(geo.s, DA_QK_DIM)
    sin64 = jnp.concatenate([-sin, sin], axis=-1).reshape(geo.s, DA_QK_DIM)
    cos_t = jnp.concatenate([jnp.tile(cos64, (1, 2)), jnp.ones((tm, HEAD_DIM), F32)], axis=0)
    sin_t = jnp.concatenate([jnp.tile(sin64, (1, 2)), jnp.zeros((tm, HEAD_DIM), F32)], axis=0)
    return cos_t, sin_t


def _qk_prep(proj, geo, gq, gk):
    tm = _pow2_tile(1024, geo.s, geo.n_ctx)
    cos_t, sin_t = _rope_tables(geo, tm)
    scale = DA_QK_DIM ** -0.5 * math.log2(math.e)
    gains = jnp.stack([jnp.tile(gq.astype(F32) * scale, 2), jnp.tile(gk.astype(F32), 2)]).reshape(2, 1, HEAD_DIM)
    n_lat_tiles, per_seq = geo.n_lat // tm, geo.s // tm
    tab = lambda i, j: (jnp.where(i < n_lat_tiles, i % per_seq, per_seq), 0)
    n_blocks = (DA_Q_W * 2) // HEAD_DIM
    return pl.pallas_call(
        _qk_prep_kernel,
        grid=(geo.m // tm, n_blocks),
        in_specs=[
            pl.BlockSpec((tm, HEAD_DIM), lambda i, j: (i, j)),
            pl.BlockSpec((1, 1, HEAD_DIM), lambda i, j: (j // DA_HEADS, 0, 0)),
            pl.BlockSpec((tm, HEAD_DIM), tab),
            pl.BlockSpec((tm, HEAD_DIM), tab),
        ],
        out_specs=pl.BlockSpec((tm, HEAD_DIM), lambda i, j: (i, j)),
        out_shape=jax.ShapeDtypeStruct((geo.m, 2 * DA_Q_W), BF16),
        compiler_params=_params(("arbitrary", "arbitrary"), 32),
        name="qk_prep",
    )(proj, gains, cos_t, sin_t)


def _fori_unrolled(n, unroll, body, init):
    def outer(j, carry):
        for u in range(unroll):
            carry = body(j * unroll + u, carry)
        return carry
    return lax.fori_loop(0, n // unroll, outer, init)


def _fold_lanes(x, op):
    out = x[:, :LANES]
    for j in range(1, x.shape[1] // LANES):
        out = op(out, x[:, j * LANES:(j + 1) * LANES])
    return out


def _attn_kernel(lam_ref, q_ref, kl_ref, kc_ref, vl_ref, vc_ref, g_ref, o_ref, s_lat, s_ctx, *, tk, n_lat_chunks,
                 nq_lat, sub_scale):
    q = q_ref[...]
    tq = q.shape[0]
    lane = lax.broadcasted_iota(jnp.int32, q.shape, 1)
    zero = jnp.zeros_like(q)
    qs = (jnp.where(lane < DA_QK_DIM, q, zero), jnp.where(lane < DA_QK_DIM, zero, q))
    nt = (((1,), (1,)), ((), ()))
    n_chunks = jnp.where(pl.program_id(2) < nq_lat, n_lat_chunks, 0)

    def scores(c, m):
        k = kl_ref[pl.ds(pl.multiple_of(c * tk, tk), tk), :]
        new = []
        for i in range(2):
            s = lax.dot_general(qs[i], k, nt, preferred_element_type=F32)
            s_lat[i, c] = s
            new.append(jnp.maximum(m[i], _fold_lanes(s, jnp.maximum)))
        return tuple(new)

    neg = jnp.full((tq, LANES), -jnp.inf, F32)
    m_lat = _fori_unrolled(n_chunks, math.gcd(n_lat_chunks, 4), scores, (neg, neg))
    m = []
    for i in range(2):
        s = lax.dot_general(qs[i], kc_ref[...], nt, preferred_element_type=F32)
        s_ctx[i] = s
        m.append(jnp.max(jnp.maximum(m_lat[i], _fold_lanes(s, jnp.maximum)), axis=-1, keepdims=True))

    def weighted(c, carry):
        v = vl_ref[pl.ds(pl.multiple_of(c * tk, tk), tk), :]
        new = []
        for i in range(2):
            p = jnp.exp2(s_lat[i, c] - m[i])
            new.append(carry[2 * i] + _fold_lanes(p, jnp.add))
            new.append(carry[2 * i + 1] + jnp.dot(p.astype(BF16), v, preferred_element_type=F32))
        return tuple(new)

    z = jnp.zeros((tq, LANES), F32)
    part = _fori_unrolled(n_chunks, math.gcd(n_lat_chunks, 4), weighted, (z, z, z, z))
    outs = []
    for i in range(2):
        p = jnp.exp2(s_ctx[i] - m[i])
        l = jnp.sum(part[2 * i] + _fold_lanes(p, jnp.add), axis=-1, keepdims=True)
        outs.append((part[2 * i + 1] + jnp.dot(p.astype(BF16), vc_ref[...], preferred_element_type=F32)) / l)
    o = outs[0] - lam_ref[0, 0] * outs[1]
    y = o * lax.rsqrt(jnp.mean(o * o, axis=-1, keepdims=True) + EPS) * g_ref[...]
    o_ref[...] = (y * sub_scale).astype(BF16)


def _diff_attention(qk, proj, geo, lam, subln_g, lam_init):
    tq = _pow2_tile(256, geo.s, geo.c)
    tk = _pow2_tile(512, geo.s)
    nq_lat, nq_ctx = geo.s // tq, geo.c // tq
    k_col0, v_col0 = DA_Q_W // HEAD_DIM, 2 * DA_Q_W // HEAD_DIM
    ctx_blk0 = geo.n_lat // geo.c

    def q_idx(b, h, qi, col0=0):
        lat = b * nq_lat + qi
        ctx = geo.n_lat // tq + b * nq_ctx + (qi - nq_lat)
        return (jnp.where(qi < nq_lat, lat, ctx), col0 + h)

    kernel = functools.partial(_attn_kernel, tk=tk, n_lat_chunks=geo.s // tk, nq_lat=nq_lat,
                               sub_scale=1.0 - lam_init)
    return pl.pallas_call(
        kernel,
        grid=(geo.b, DA_HEADS, nq_lat + nq_ctx),
        in_specs=[
            pl.BlockSpec(memory_space=pltpu.SMEM),
            pl.BlockSpec((tq, HEAD_DIM), q_idx),
            pl.BlockSpec((geo.s, HEAD_DIM), lambda b, h, qi: (b, k_col0 + h)),
            pl.BlockSpec((geo.c, HEAD_DIM), lambda b, h, qi: (ctx_blk0 + b, k_col0 + h)),
            pl.BlockSpec((geo.s, HEAD_DIM), lambda b, h, qi: (b, v_col0 + h)),
            pl.BlockSpec((geo.c, HEAD_DIM), lambda b, h, qi: (ctx_blk0 + b, v_col0 + h)),
            pl.BlockSpec((1, HEAD_DIM), lambda b, h, qi: (0, 0)),
        ],
        out_specs=pl.BlockSpec((tq, HEAD_DIM), q_idx),
        out_shape=jax.ShapeDtypeStruct((geo.m, DA_V_W), BF16),
        scratch_shapes=[pltpu.VMEM((2, geo.s // tk, tq, tk), F32), pltpu.VMEM((2, tq, geo.c), F32)],
        compiler_params=_params(("arbitrary", "arbitrary", "arbitrary"), 40),
        name="diff_attn",
    )(lam.reshape(1, 1), qk, qk, qk, proj, proj, subln_g.reshape(1, HEAD_DIM).astype(F32))


def _gelu(x):
    return 0.5 * x * (1.0 + lax.erf(x * (1.0 / math.sqrt(2.0))))


def _gmlp_kernel(u_ref, v_ref, lng_ref, lnb_ref, ws_ref, bs_ref, o_ref):
    tm = u_ref.shape[0]
    for r in range(tm // GM_CHUNK):
        rows = slice(r * GM_CHUNK, (r + 1) * GM_CHUNK)
        for g in range(GM_GROUPS):
            cols = slice(g * HEAD_DIM, (g + 1) * HEAD_DIM)
            v = _gelu(v_ref[rows, cols].astype(F32))
            mu = jnp.mean(v, axis=-1, keepdims=True)
            var = jnp.mean(jnp.square(v - mu), axis=-1, keepdims=True)
            vn = (v - mu) * lax.rsqrt(var + EPS) * lng_ref[:, cols] + lnb_ref[:, cols]
            s = jnp.dot(ws_ref[g], vn.astype(BF16), preferred_element_type=F32) + bs_ref[:, cols]
            o_ref[rows, cols] = (_gelu(u_ref[rows, cols].astype(F32)) * s).astype(BF16)


def _gmlp(proj, geo, ln_g, ln_b, w_s, b_s):
    tm = _pow2_tile(256, geo.s, geo.c)
    u_blk = (2 * DA_Q_W + DA_V_W) // GM_W
    bs_full = jnp.repeat(b_s.astype(F32).T, HEAD_DIM, axis=1)
    return pl.pallas_call(
        _gmlp_kernel,
        grid=(geo.m // tm,),
        in_specs=[
            pl.BlockSpec((tm, GM_W), lambda i: (i, u_blk)),
            pl.BlockSpec((tm, GM_W), lambda i: (i, u_blk + 1)),
            pl.BlockSpec((1, GM_W), lambda i: (0, 0)),
            pl.BlockSpec((1, GM_W), lambda i: (0, 0)),
            pl.BlockSpec((GM_GROUPS, GM_CHUNK, GM_CHUNK), lambda i: (0, 0, 0)),
            pl.BlockSpec((GM_CHUNK, GM_W), lambda i: (0, 0)),
        ],
        out_specs=pl.BlockSpec((tm, GM_W), lambda i: (i, 0)),
        out_shape=jax.ShapeDtypeStruct((geo.m, GM_W), BF16),
        compiler_params=_params(("arbitrary",), 32),
        name="gmlp",
    )(proj, proj, ln_g.reshape(1, GM_W).astype(F32), ln_b.reshape(1, GM_W).astype(F32), w_s.astype(BF16), bs_full)


def _pack_bf16_pair(lo, hi):
    lo_bits = lax.bitcast_convert_type(lo.astype(BF16).astype(F32), U32)
    hi_bits = lax.bitcast_convert_type(hi.astype(BF16).astype(F32), U32)
    return (lo_bits >> 16) | (hi_bits & jnp.uint32(0xFFFF0000))


def _store_token_major(ref, packed):
    rows, width = packed.shape
    k = width // LANES
    for c in range(k):
        ref[pl.ds(c, rows, stride=k), :] = packed[:, c * LANES:(c + 1) * LANES]


def _load_token_major(ref, row0, rows, k, c):
    return ref[pl.ds(row0 * k + c, rows, stride=k), :]


def _resid_norm_route(y, x_ref, gate_ref, g_ref, sh_ref, sc_ref, rw_ref, rb_ref, xn_ref, hp_ref, lg_ref):
    xn = x_ref[...] + gate_ref[0] * y
    xn_ref[...] = xn
    h = _modulated_norm(xn, g_ref[...], sh_ref[0], sc_ref[0])
    lg_ref[...] = jnp.dot(h, rw_ref[...], preferred_element_type=F32, precision=HIGHEST) + rb_ref[...]
    half = h.shape[1] // 2
    _store_token_major(hp_ref, _pack_bf16_pair(h[:, :half], h[:, half:]))


def _epilogue_specs(geo, tm, d, grid_rank):
    def ix(f):
        return (lambda i: f(i)) if grid_rank == 1 else (lambda i, k: f(i))
    mod = lambda k: ix(lambda i: (geo.mod_row(i, tm), 0, k))
    in_specs = [
        pl.BlockSpec((tm, d), ix(lambda i: (i, 0))),
        pl.BlockSpec((1, 1, d), mod(2)),
        pl.BlockSpec((1, d), ix(lambda i: (0, 0))),
        pl.BlockSpec((1, 1, d), mod(3)),
        pl.BlockSpec((1, 1, d), mod(4)),
        pl.BlockSpec((d, ROUTER_PAD), ix(lambda i: (0, 0))),
        pl.BlockSpec((1, ROUTER_PAD), ix(lambda i: (0, 0))),
    ]
    out_specs = [
        pl.BlockSpec((tm, d), ix(lambda i: (i, 0))),
        pl.BlockSpec((tm * (d // 2 // LANES), LANES), ix(lambda i: (i, 0))),
        pl.BlockSpec((tm, ROUTER_PAD), ix(lambda i: (i, 0))),
    ]
    return in_specs, out_specs


def _epilogue_out_shapes(n_rows, d):
    return [jax.ShapeDtypeStruct((n_rows, d), F32), jax.ShapeDtypeStruct((n_rows * (d // 2 // LANES), LANES), U32),
            jax.ShapeDtypeStruct((n_rows, ROUTER_PAD), F32)]


def _router_operands(router_w, router_b):
    d = router_w.shape[0]
    rw = jnp.zeros((d, ROUTER_PAD), F32).at[:, :N_EXPERTS].set(router_w.astype(F32))
    rb = jnp.zeros((1, ROUTER_PAD), F32).at[0, :N_EXPERTS].set(router_b.astype(F32))
    return rw, rb


def _attn_out_kernel(a_ref, gm_ref, wo_ref, x_ref, gate_ref, g_ref, sh_ref, sc_ref, rw_ref, rb_ref,
                     xn_ref, hp_ref, lg_ref):
    y = jnp.dot(a_ref[...], wo_ref[:DA_V_W, :], preferred_element_type=F32)
    y = y + jnp.dot(gm_ref[...], wo_ref[DA_V_W:, :], preferred_element_type=F32)
    _resid_norm_route(y, x_ref, gate_ref, g_ref, sh_ref, sc_ref, rw_ref, rb_ref, xn_ref, hp_ref, lg_ref)


def _attn_out(attn, gm, w_o, xa, geo, mod, g2, rw, rb):
    d = xa.shape[1]
    tm = _pow2_tile(512, geo.s, geo.n_ctx)
    ep_in, ep_out = _epilogue_specs(geo, tm, d, 1)
    return pl.pallas_call(
        _attn_out_kernel,
        grid=(geo.m // tm,),
        in_specs=[
            pl.BlockSpec((tm, DA_V_W), lambda i: (i, 0)),
            pl.BlockSpec((tm, GM_W), lambda i: (i, 0)),
            pl.BlockSpec((DA_V_W + GM_W, d), lambda i: (0, 0)),
        ] + ep_in,
        out_specs=ep_out,
        out_shape=_epilogue_out_shapes(geo.m, d),
        compiler_params=_params(("arbitrary",), 52),
        name="attn_out",
    )(attn, gm, w_o, xa, mod, g2.reshape(1, d), mod, mod, rw, rb)


def _route(logits, n_tok):
    top_val, top_idx = lax.top_k(logits[:, :N_EXPERTS], TOP_K)
    gates = jax.nn.softmax(top_val, axis=-1)
    n_asg = n_tok * TOP_K
    n_tiles = n_asg // MOE_TM + N_EXPERTS
    flat_e = top_idx.reshape(-1).astype(jnp.int32)
    order = jnp.argsort(flat_e, stable=True).astype(jnp.int32)
    experts = jnp.arange(N_EXPERTS, dtype=jnp.int32)
    counts = jnp.sum((flat_e[:, None] == experts[None, :]).astype(jnp.int32), axis=0)
    grp_start = jnp.cumsum(counts) - counts
    tiles_e = (counts + MOE_TM - 1) // MOE_TM
    tile_end = jnp.cumsum(tiles_e)
    tile_start = tile_end - tiles_e
    n_valid = tile_end[-1]
    t = jnp.arange(n_tiles, dtype=jnp.int32)
    t_eff = jnp.minimum(t, n_valid - 1)
    tile_e = jnp.minimum(jnp.searchsorted(tile_end, t_eff, side="right"), N_EXPERTS - 1).astype(jnp.int32)
    tile_off = (t_eff - tile_start[tile_e]) * MOE_TM
    tile_rows = jnp.where(t < n_valid, jnp.clip(counts[tile_e] - tile_off, 0, MOE_TM), 0).astype(jnp.int32)
    r = jnp.arange(MOE_TM, dtype=jnp.int32)[None, :]
    order_pad = jnp.concatenate([order, jnp.zeros((MOE_TM,), jnp.int32)])
    seg = jax.vmap(lambda s0: lax.dynamic_slice(order_pad, (s0,), (MOE_TM,)))(grp_start[tile_e] + tile_off)
    asg = jnp.where(r < tile_rows[:, None], seg, 0)
    row_tok = (asg // TOP_K).reshape(n_tiles, 1, MOE_TM)
    row_slot = ((asg % TOP_K) * n_tok + asg // TOP_K).reshape(n_tiles, 1, MOE_TM)
    return tile_e, tile_rows, row_tok, row_slot, gates


def _unpack_bf16_pair(u):
    return lax.bitcast_convert_type(u << 16, F32), lax.bitcast_convert_type(u & jnp.uint32(0xFFFF0000), F32)


def _row_blocks(n_rows):
    return (n_rows + MOE_RB - 1) // MOE_RB


def _padded_blocks(n_blocks):
    return (n_blocks + MOE_P1_BLOCKS - 1) // MOE_P1_BLOCKS * MOE_P1_BLOCKS


def _moe_kernel(te_ref, tr_ref, tok_ref, tok_next_ref, slot_ref, hp_hbm, w1g_ref, w1l_ref, b1g_ref, b1l_ref,
                w2_ref, b2_ref, ys_hbm, xbuf, xs, act, w2b, ystage, gsem, ssem):
    t, s = pl.program_id(0), pl.program_id(1)
    n_tiles = pl.num_programs(0)
    rows = tr_ref[t]
    nrb = _row_blocks(rows)
    nf = act.shape[0]
    half = xs.shape[1] // 2
    kt = half // LANES
    blk_rows = MOE_RB * kt
    grp = MOE_RB // SUBLANES
    prefetched = jnp.where(t > 0, _padded_blocks(_row_blocks(tr_ref[jnp.maximum(t - 1, 0)])), 0)

    def gather_start(rows_ref, r_dyn, u):
        dst = pl.multiple_of(r_dyn * (SUBLANES * kt), SUBLANES * kt) + u * kt
        pltpu.make_async_copy(hp_hbm.at[rows_ref[0, 0, r_dyn * SUBLANES + u]], xbuf.at[pl.ds(dst, kt), :],
                              gsem.at[0]).start()

    def gather_wait(n_blocks):
        def wait_block(i, _):
            blk = xbuf.at[pl.ds(pl.multiple_of(i * blk_rows, blk_rows), blk_rows), :]
            pltpu.make_async_copy(blk, blk, gsem.at[0]).wait()
            return 0
        lax.fori_loop(0, n_blocks, wait_block, 0)

    @pl.when(s == 0)
    def _():
        def top_up(j, _):
            for u in range(SUBLANES):
                gather_start(tok_ref, j, u)
            return 0
        lax.fori_loop(prefetched * grp, nrb * grp, top_up, 0)
        gather_wait(jnp.maximum(prefetched, nrb))

        def unpack_block(i, _):
            r0 = pl.multiple_of(i * MOE_RB, MOE_RB)
            for c in range(kt):
                lo, hi = _unpack_bf16_pair(_load_token_major(xbuf, r0, MOE_RB, kt, c))
                xs[pl.ds(r0, MOE_RB), c * LANES:(c + 1) * LANES] = lo.astype(BF16)
                xs[pl.ds(r0, MOE_RB), half + c * LANES:half + (c + 1) * LANES] = hi.astype(BF16)
            return 0
        lax.fori_loop(0, nrb, unpack_block, 0)

        def zero_block(i, _):
            xs[pl.ds(pl.multiple_of(i * MOE_RB, MOE_RB), MOE_RB), :] = jnp.zeros((MOE_RB, xs.shape[1]), BF16)
            return 0
        lax.fori_loop(nrb, _padded_blocks(nrb), zero_block, 0)

    @pl.when(jnp.logical_and(s < nf, nrb > 0))
    def _():
        w1g = w1g_ref[0, 0].astype(BF16)
        w1l = w1l_ref[0, 0].astype(BF16)
        w2b[pl.ds(pl.multiple_of(s * MOE_TF, MOE_TF), MOE_TF), :] = w2_ref[0, 0].astype(BF16)

        n_padded = _padded_blocks(nrb)
        share = grp // nf

        def iteration(it, _):
            for b in range(MOE_P1_BLOCKS):
                i = it * MOE_P1_BLOCKS + b
                r0 = pl.multiple_of(i * MOE_RB, MOE_RB)
                g0 = (s * n_padded + i) * share
                for j in range(share):
                    for u in range(SUBLANES):
                        gather_start(tok_next_ref, g0 + j, u)
                x = xs[pl.ds(r0, MOE_RB), :]
                g = jnp.minimum(jnp.dot(x, w1g, preferred_element_type=F32) + b1g_ref[0, 0], SWIGLU_LIMIT)
                lin = jnp.clip(jnp.dot(x, w1l, preferred_element_type=F32) + b1l_ref[0, 0],
                               -SWIGLU_LIMIT, SWIGLU_LIMIT)
                act[s, pl.ds(r0, MOE_RB), :] = (g * jax.nn.sigmoid(SWIGLU_ALPHA * g) * (lin + 1.0)).astype(BF16)
            return 0
        lax.fori_loop(0, n_padded // MOE_P1_BLOCKS, iteration, 0)

    def scatter_start(stage_slot, r0, g, u):
        src = g * (SUBLANES * kt) + u * kt
        if not isinstance(g, int):
            src = pl.multiple_of(g * (SUBLANES * kt), SUBLANES * kt) + u * kt
        pltpu.make_async_copy(ystage.at[stage_slot, pl.ds(src, kt), :],
                              ys_hbm.at[slot_ref[0, 0, r0 + g * SUBLANES + u]],
                              ssem.at[stage_slot]).start(priority=u % 2)

    def scatter_wait(stage_slot, n_rows):
        blk = ystage.at[stage_slot, pl.ds(0, n_rows * kt), :]
        pltpu.make_async_copy(blk, blk, ssem.at[stage_slot]).wait()

    @pl.when(jnp.logical_and(s == nf, nrb > 0))
    def _():
        def staged(i):
            r0 = pl.multiple_of(i * MOE_RB, MOE_RB)
            a = jnp.concatenate([act[f, pl.ds(r0, MOE_RB), :] for f in range(nf)], axis=1)
            y = jnp.dot(a, w2b[...], preferred_element_type=F32) + b2_ref[0, 0]
            return _pack_bf16_pair(y[:, :half], y[:, half:])

        _store_token_major(ystage.at[0], staged(0))

        def block(i, _):
            stage_slot = i % 2
            r_prev = pl.multiple_of((i - 1) * MOE_RB, MOE_RB)
            for g in range(grp):
                for u in range(SUBLANES):
                    scatter_start(1 - stage_slot, r_prev, g, u)
            packed = staged(i)

            @pl.when(i >= 2)
            def _():
                scatter_wait(stage_slot, MOE_RB)

            _store_token_major(ystage.at[stage_slot], packed)
            return 0
        lax.fori_loop(1, nrb, block, 0)

        last = nrb - 1
        last_slot = last % 2
        r_last = pl.multiple_of(last * MOE_RB, MOE_RB)
        n_last = rows - r_last

        def issue_group(g, _):
            for u in range(SUBLANES):
                scatter_start(last_slot, r_last, g, u)
            return 0
        lax.fori_loop(0, n_last // SUBLANES, issue_group, 0)
        for u in range(SUBLANES - 1):
            @pl.when(u < n_last % SUBLANES)
            def _():
                scatter_start(last_slot, r_last, n_last // SUBLANES, u)

        @pl.when(nrb >= 2)
        def _():
            scatter_wait(1 - last_slot, MOE_RB)

        def wait_one(r, _):
            scatter_wait(last_slot, 1)
            return 0
        lax.fori_loop(0, n_last, wait_one, 0)

    @pl.when(jnp.logical_and(t == n_tiles - 1, s == nf))
    def _():
        gather_wait(_padded_blocks(nrb))


def _moe_experts(hp, n_tok, tile_e, tile_rows, row_tok, row_slot, w1, b1, w2, b2, li):
    n_tiles = row_tok.shape[0]
    d = w2.shape[-1]
    nf = D_EXPERT // MOE_TF
    kt = d // 2 // LANES
    fz = lambda s, tr, t: jnp.where(tr[t] > 0, jnp.minimum(s, nf - 1), nf - 1)
    b1r = b1.reshape(b1.shape[0], N_EXPERTS, 1, 2 * D_EXPERT)
    b2r = b2.reshape(b2.shape[0], N_EXPERTS, 1, d)
    rows_spec = lambda f: pl.BlockSpec((1, 1, MOE_TM), f, memory_space=pltpu.SMEM)
    grid_spec = pltpu.PrefetchScalarGridSpec(
        num_scalar_prefetch=2,
        grid=(n_tiles, nf + 1),
        in_specs=[
            rows_spec(lambda t, s, te, tr: (t, 0, 0)),
            rows_spec(lambda t, s, te, tr: (jnp.minimum(t + 1, n_tiles - 1), 0, 0)),
            rows_spec(lambda t, s, te, tr: (t, 0, 0)),
            pl.BlockSpec(memory_space=pl.ANY),
            pl.BlockSpec((1, 1, d, MOE_TF), lambda t, s, te, tr: (li, te[t], 0, fz(s, tr, t))),
            pl.BlockSpec((1, 1, d, MOE_TF), lambda t, s, te, tr: (li, te[t], 0, nf + fz(s, tr, t))),
            pl.BlockSpec((1, 1, 1, MOE_TF), lambda t, s, te, tr: (li, te[t], 0, fz(s, tr, t))),
            pl.BlockSpec((1, 1, 1, MOE_TF), lambda t, s, te, tr: (li, te[t], 0, nf + fz(s, tr, t))),
            pl.BlockSpec((1, 1, MOE_TF, d), lambda t, s, te, tr: (li, te[t], fz(s, tr, t), 0)),
            pl.BlockSpec((1, 1, 1, d), lambda t, s, te, tr: (li, te[t], 0, 0)),
        ],
        out_specs=pl.BlockSpec(memory_space=pl.ANY),
        scratch_shapes=[
            pltpu.VMEM((MOE_TM * kt, LANES), U32),
            pltpu.VMEM((MOE_TM, d), BF16),
            pltpu.VMEM((nf, MOE_TM, MOE_TF), BF16),
            pltpu.VMEM((D_EXPERT, d), BF16),
            pltpu.VMEM((2, MOE_RB * kt, LANES), U32),
            pltpu.SemaphoreType.DMA((1,)),
            pltpu.SemaphoreType.DMA((2,)),
        ],
    )
    return pl.pallas_call(
        _moe_kernel,
        grid_spec=grid_spec,
        out_shape=jax.ShapeDtypeStruct((TOP_K * n_tok, kt, LANES), U32),
        compiler_params=_params(("arbitrary", "arbitrary"), 60),
        name="moe_experts",
    )(tile_e, tile_rows, row_tok, row_tok, row_slot, hp.reshape(-1, kt, LANES), w1, w1, b1r, b1r, w2, b2r)


def _combine_kernel(ys_ref, gate_ref, x_ref, g5_ref, o_ref):
    tc, d = x_ref.shape
    half = d // 2
    kt = half // LANES
    gate = gate_ref[...]
    gates = [jnp.broadcast_to(gate[:, k:k + 1], (tc, LANES)) for k in range(TOP_K)]
    for c in range(kt):
        acc_lo = acc_hi = None
        for k in range(TOP_K):
            lo, hi = _unpack_bf16_pair(_load_token_major(ys_ref.at[k], 0, tc, kt, c))
            acc_lo = gates[k] * lo if k == 0 else acc_lo + gates[k] * lo
            acc_hi = gates[k] * hi if k == 0 else acc_hi + gates[k] * hi
        lo_cols = slice(c * LANES, (c + 1) * LANES)
        hi_cols = slice(half + c * LANES, half + (c + 1) * LANES)
        o_ref[:, lo_cols] = x_ref[:, lo_cols] + g5_ref[0, :, lo_cols] * acc_lo
        o_ref[:, hi_cols] = x_ref[:, hi_cols] + g5_ref[0, :, hi_cols] * acc_hi


def _moe_combine(ys, gates, xn, geo, n_tok, mod):
    d = xn.shape[1]
    kt = d // 2 // LANES
    tc = _pow2_tile(COMBINE_TC, geo.s, geo.n_ctx)
    return pl.pallas_call(
        _combine_kernel,
        grid=(n_tok // tc,),
        in_specs=[
            pl.BlockSpec((TOP_K, tc * kt, LANES), lambda i: (0, i, 0)),
            pl.BlockSpec((tc, TOP_K), lambda i: (i, 0)),
            pl.BlockSpec((tc, d), lambda i: (i, 0)),
            pl.BlockSpec((1, 1, d), lambda i: (geo.mod_row(i, tc), 0, 5)),
        ],
        out_specs=pl.BlockSpec((tc, d), lambda i: (i, 0)),
        out_shape=jax.ShapeDtypeStruct((n_tok, d), F32),
        compiler_params=_params(("arbitrary",), 40),
        name="moe_combine",
    )(ys.reshape(TOP_K, n_tok * kt, LANES), gates, xn, mod)


def _moe_layer(xn, hp, logits, geo, n_tok, mod, w1, b1, w2, b2, li):
    tile_e, tile_rows, row_tok, row_slot, gates = _route(logits[:n_tok], n_tok)
    ys = _moe_experts(hp, n_tok, tile_e, tile_rows, row_tok, row_slot, w1, b1, w2, b2, li)
    return _moe_combine(ys, gates, xn, geo, n_tok, mod)


def _ssd_prep_kernel(raw_ref, bias_ref, aneg_ref, dt_ref, ac_ref):
    tm = raw_ref.shape[0]
    li = lax.broadcasted_iota(jnp.int32, (SSD_CHUNK, SSD_CHUNK), 0)
    si = lax.broadcasted_iota(jnp.int32, (SSD_CHUNK, SSD_CHUNK), 1)
    tri_f = (si <= li).astype(F32)
    tri_b = (si >= li).astype(F32)
    lane = lax.broadcasted_iota(jnp.int32, (SSD_CHUNK, SSD_DT_W), 1)
    for c in range(tm // SSD_CHUNK):
        rows = slice(c * SSD_CHUNK, (c + 1) * SSD_CHUNK)
        v = raw_ref[rows, :] + bias_ref[...]
        dt = jnp.maximum(v, 0.0) + jnp.log1p(jnp.exp(-jnp.abs(v)))
        a = dt * aneg_ref[...]
        fwd = jnp.dot(tri_f, a, preferred_element_type=F32, precision=HIGHEST)
        bwd = jnp.dot(tri_b, a, preferred_element_type=F32, precision=HIGHEST)
        dt_ref[rows, :] = dt
        ac_ref[rows, :] = jnp.where(lane < SSD_HEADS, fwd, bwd)


def _ssd_prep(dt_raw, geo, dt_bias, a_log):
    tm = _pow2_tile(512, geo.s, geo.c)
    a_neg = -jnp.exp(a_log.astype(F32)).reshape(1, SSD_DT_W)
    spec = pl.BlockSpec((tm, SSD_DT_W), lambda i: (i, 0))
    vec = pl.BlockSpec((1, SSD_DT_W), lambda i: (0, 0))
    return pl.pallas_call(
        _ssd_prep_kernel,
        grid=(geo.m // tm,),
        in_specs=[spec, vec, vec],
        out_specs=[spec, spec],
        out_shape=[jax.ShapeDtypeStruct((geo.m, SSD_DT_W), F32)] * 2,
        compiler_params=_params(("arbitrary",), 32),
        name="ssd_prep",
    )(dt_raw, dt_bias.astype(F32).reshape(1, SSD_DT_W), a_neg)


CONV_HALO = 16


def _conv_kernel(prev_ref, cur_ref, next_ref, w_ref, b_ref, o_ref, ext, *, geo, tm):
    i = pl.program_id(0)
    n_lat_tiles = geo.n_lat // tm
    per_lat, per_ctx = geo.s // tm, geo.c // tm
    lat = i < n_lat_tiles
    p = jnp.where(lat, i % per_lat, (i - n_lat_tiles) % per_ctx)
    first = p == 0
    last = jnp.where(lat, p == per_lat - 1, p == per_ctx - 1)
    ext[0:CONV_HALO, :] = jnp.where(first, 0.0, prev_ref[...].astype(F32))
    ext[CONV_HALO:CONV_HALO + tm, :] = cur_ref[...].astype(F32)
    ext[CONV_HALO + tm:, :] = jnp.where(last, 0.0, next_ref[...].astype(F32))
    pad = SSD_CONV_W // 2
    acc = w_ref[0:1, :] * ext[pl.ds(CONV_HALO - pad, tm), :]
    for k in range(1, SSD_CONV_W):
        acc = acc + w_ref[k:k + 1, :] * ext[pl.ds(CONV_HALO - pad + k, tm), :]
    acc = acc + b_ref[...]
    o_ref[...] = (acc * jax.nn.sigmoid(acc)).astype(BF16)


def _conv_silu(zx, geo, conv_w, conv_b):
    tm = _pow2_tile(256, geo.s, geo.c)
    tc = 1024
    col0 = SSD_D_INNER // tc
    hb = tm // CONV_HALO
    n_halo_blocks = geo.m // CONV_HALO
    return pl.pallas_call(
        functools.partial(_conv_kernel, geo=geo, tm=tm),
        grid=(geo.m // tm, SSD_CONV_CH // tc),
        in_specs=[
            pl.BlockSpec((CONV_HALO, tc), lambda i, j: (jnp.maximum(i * hb - 1, 0), col0 + j)),
            pl.BlockSpec((tm, tc), lambda i, j: (i, col0 + j)),
            pl.BlockSpec((CONV_HALO, tc), lambda i, j: (jnp.minimum((i + 1) * hb, n_halo_blocks - 1), col0 + j)),
            pl.BlockSpec((SSD_CONV_W, tc), lambda i, j: (0, j)),
            pl.BlockSpec((1, tc), lambda i, j: (0, j)),
        ],
        out_specs=pl.BlockSpec((tm, tc), lambda i, j: (i, j)),
        out_shape=jax.ShapeDtypeStruct((geo.m, SSD_CONV_CH), BF16),
        scratch_shapes=[pltpu.VMEM((tm + 2 * CONV_HALO, tc), F32)],
        compiler_params=_params(("arbitrary", "arbitrary"), 32),
        name="ssd_conv",
    )(zx, zx, zx, conv_w.astype(F32), conv_b.astype(F32).reshape(1, SSD_CONV_CH))


def _ssd_scan_kernel(xs_ref, b_ref, c_ref, dt_ref, ac_ref, y_ref, state, xg, bt_g, c_g, dt_g, ac_g, ac_t, y_g):
    d, s = pl.program_id(1), pl.program_id(2)
    q = SSD_CHUNK
    is_fwd = d == 0

    @pl.when(s == 0)
    def _():
        state[...] = jnp.zeros(state.shape, F32)

    dt2, ac2 = dt_ref[...], ac_ref[...]
    dtc = jnp.where(is_fwd, dt2[:, :SSD_HEADS], dt2[:, SSD_HEADS:])
    acc = jnp.where(is_fwd, ac2[:, :SSD_HEADS], ac2[:, SSD_HEADS:])
    ac_t[...] = ac2.T
    b_f32 = b_ref[...].astype(F32)
    for g in range(SSD_GROUPS):
        xg[g] = xs_ref[:, g * SSD_GROUP_W:(g + 1) * SSD_GROUP_W]
        bt_g[g] = b_f32[:, g * SSD_STATE:(g + 1) * SSD_STATE].T.astype(BF16)
        c_g[g] = c_ref[:, g * SSD_STATE:(g + 1) * SSD_STATE]
        dt_g[g] = dtc[:, g * SSD_HPG:(g + 1) * SSD_HPG]
        ac_g[g] = acc[:, g * SSD_HPG:(g + 1) * SSD_HPG]

    li = lax.broadcasted_iota(jnp.int32, (q, q), 0)
    si = lax.broadcasted_iota(jnp.int32, (q, q), 1)
    mask = jnp.where(is_fwd, li - si, si - li) >= 0
    lane = lax.broadcasted_iota(jnp.int32, (q, LANES), 1)
    lo_half = lane < SSD_HEAD_DIM
    lo_row = lo_half[0:1, :]

    def group(g, _):
        cg = c_g[g]
        cb = jnp.dot(cg, bt_g[g], preferred_element_type=F32)
        dtg, acg = dt_g[g], ac_g[g]
        xf = xg[g].astype(F32)
        a_cols, d_cols, decays = [], [], []
        for j in range(SSD_HPG):
            a_col = jnp.broadcast_to(acg[:, j:j + 1], (q, q))
            a_row = ac_t[pl.ds(d * SSD_HEADS + g * SSD_HPG + j, 1), :]
            seg = a_col - a_row
            decays.append((cb * jnp.exp(jnp.where(mask, seg, -jnp.inf))).astype(BF16))
            a_cols.append(a_col)
            d_cols.append(jnp.broadcast_to(dtg[:, j:j + 1], (q, q)))
        for p in range(SSD_HPG // 2):
            cols = slice(p * LANES, (p + 1) * LANES)
            acx = jnp.where(lo_half, a_cols[2 * p], a_cols[2 * p + 1])
            dtx = jnp.where(lo_half, d_cols[2 * p], d_cols[2 * p + 1])
            a_last = jnp.where(is_fwd, acx[q - 1:q, :], acx[0:1, :])
            xdt = xf[:, cols] * dtx
            xdt_b = xdt.astype(BF16)
            zero = jnp.zeros_like(xdt_b)
            rhs = jnp.concatenate([jnp.where(lo_half, xdt_b, zero), jnp.where(lo_half, zero, xdt_b)], axis=0)
            lhs = jnp.concatenate([decays[2 * p], decays[2 * p + 1]], axis=1)
            st = state[g, :, cols]
            y = jnp.dot(lhs, rhs, preferred_element_type=F32)
            y = y + jnp.dot(cg, st.astype(BF16), preferred_element_type=F32) * jnp.exp(acx)
            y_g[g, :, cols] = y
            wx = (xdt * jnp.exp(a_last - acx)).astype(BF16)
            state[g, :, cols] = st * jnp.exp(a_last) + jnp.dot(bt_g[g], wx, preferred_element_type=F32)
        return 0

    lax.fori_loop(0, SSD_GROUPS, group, 0)
    for g in range(SSD_GROUPS):
        y_ref[0, :, g * SSD_GROUP_W:(g + 1) * SSD_GROUP_W] = y_g[g].astype(BF16)


def _ssd_scan(cv, dt, ac, geo):
    q = SSD_CHUNK
    ncc, nlc = geo.c // q, geo.s // q
    bc_col0 = SSD_D_INNER // (SSD_GROUPS * SSD_STATE)

    def lat_chunk(d, s):
        c = jnp.maximum(s - ncc, 0)
        return jnp.where(d == 0, c, nlc - 1 - c)

    def row_blk(b, d, s):
        ctx_c = jnp.where(d == 0, s, ncc - 1 - s)
        return jnp.where(s < ncc, geo.b * nlc + b * ncc + ctx_c, b * nlc + lat_chunk(d, s))

    return pl.pallas_call(
        _ssd_scan_kernel,
        grid=(geo.b, 2, ncc + nlc),
        in_specs=[
            pl.BlockSpec((q, SSD_D_INNER), lambda b, d, s: (row_blk(b, d, s), 0)),
            pl.BlockSpec((q, SSD_GROUPS * SSD_STATE), lambda b, d, s: (row_blk(b, d, s), bc_col0 + 2 * d)),
            pl.BlockSpec((q, SSD_GROUPS * SSD_STATE), lambda b, d, s: (row_blk(b, d, s), bc_col0 + 2 * d + 1)),
            pl.BlockSpec((q, SSD_DT_W), lambda b, d, s: (row_blk(b, d, s), 0)),
            pl.BlockSpec((q, SSD_DT_W), lambda b, d, s: (row_blk(b, d, s), 0)),
        ],
        out_specs=pl.BlockSpec((1, q, SSD_D_INNER), lambda b, d, s: (d, b * nlc + lat_chunk(d, s), 0)),
        out_shape=jax.ShapeDtypeStruct((2, geo.n_lat, SSD_D_INNER), BF16),
        scratch_shapes=[
            pltpu.VMEM((SSD_GROUPS, SSD_STATE, SSD_GROUP_W), F32),
            pltpu.VMEM((SSD_GROUPS, q, SSD_GROUP_W), BF16),
            pltpu.VMEM((SSD_GROUPS, SSD_STATE, q), BF16),
            pltpu.VMEM((SSD_GROUPS, q, SSD_STATE), BF16),
            pltpu.VMEM((SSD_GROUPS, q, SSD_HPG), F32),
            pltpu.VMEM((SSD_GROUPS, q, SSD_HPG), F32),
            pltpu.VMEM((SSD_DT_W, q), F32),
            pltpu.VMEM((SSD_GROUPS, q, SSD_GROUP_W), F32),
        ],
        compiler_params=_params(("arbitrary", "arbitrary", "arbitrary"), 40),
        name="ssd_scan",
    )(cv, cv, cv, dt, ac)


def _ssd_out_kernel(yf_ref, yb_ref, xs_ref, z_ref, skip_ref, ng_ref, wo_ref, x_ref, gate_ref, g_ref, sh_ref,
                    sc_ref, rw_ref, rb_ref, xn_ref, hp_ref, lg_ref, acc):
    k = pl.program_id(1)
    y = yf_ref[0].astype(F32) + yb_ref[0].astype(F32) + skip_ref[...] * xs_ref[...].astype(F32)
    z = z_ref[...].astype(F32)
    t = y * (z * jax.nn.sigmoid(z))
    parts = []
    for gg in range(t.shape[1] // SSD_GROUP_W):
        tg = t[:, gg * SSD_GROUP_W:(gg + 1) * SSD_GROUP_W]
        parts.append(tg * lax.rsqrt(jnp.mean(tg * tg, axis=-1, keepdims=True) + EPS))
    tn = (jnp.concatenate(parts, axis=1) * ng_ref[...]).astype(BF16)
    contrib = jnp.dot(tn, wo_ref[...], preferred_element_type=F32)

    @pl.when(k == 0)
    def _():
        acc[...] = contrib

    @pl.when(k > 0)
    def _():
        acc[...] += contrib

    @pl.when(k == pl.num_programs(1) - 1)
    def _():
        _resid_norm_route(acc[...], x_ref, gate_ref, g_ref, sh_ref, sc_ref, rw_ref, rb_ref, xn_ref, hp_ref, lg_ref)


def _ssd_out(yd, cv, zx, skip_x, norm_g, w_o, xa, geo, mod, g2, rw, rb):
    d = xa.shape[1]
    tm = _pow2_tile(512, geo.s)
    tk = 1024
    ep_in, ep_out = _epilogue_specs(geo, tm, d, 2)
    return pl.pallas_call(
        _ssd_out_kernel,
        grid=(geo.n_lat // tm, SSD_D_INNER // tk),
        in_specs=[
            pl.BlockSpec((1, tm, tk), lambda i, k: (0, i, k)),
            pl.BlockSpec((1, tm, tk), lambda i, k: (1, i, k)),
            pl.BlockSpec((tm, tk), lambda i, k: (i, k)),
            pl.BlockSpec((tm, tk), lambda i, k: (i, k)),
            pl.BlockSpec((1, tk), lambda i, k: (0, k)),
            pl.BlockSpec((1, tk), lambda i, k: (0, k)),
            pl.BlockSpec((tk, d), lambda i, k: (k, 0)),
        ] + ep_in,
        out_specs=ep_out,
        out_shape=_epilogue_out_shapes(geo.n_lat, d),
        scratch_shapes=[pltpu.VMEM((tm, d), F32)],
        compiler_params=_params(("arbitrary", "arbitrary"), 52),
        name="ssd_out",
    )(yd, yd, cv, zx, skip_x, norm_g.astype(F32).reshape(1, SSD_D_INNER), w_o, xa, mod, g2.reshape(1, d), mod, mod,
      rw, rb)


def kernel(x, c, ctx, c_ctx, ada_w, ada_b, norm_g, attn_w_in, attn_w_o, da_q_norm_g, da_k_norm_g, da_lambda,
           da_subln_g, gm_ln_g, gm_ln_b, gm_w_s, gm_b_s, ssd_w_in, ssd_conv_w, ssd_conv_b, ssd_dt_bias, ssd_a_log,
           ssd_d_skip, ssd_norm_g, ssd_w_o, router_w, router_b, moe_w1, moe_b1, moe_w2, moe_b2):
    bsz, n, d = x.shape
    geo = _Geo(bsz, n, ctx.shape[1])
    assert ada_w.shape[0] == 2 and d == D_MODEL, "even (attention/gMLP) layer followed by a final odd (SSD) layer"

    n_rows = -(-(bsz + 1) // 8) * 8
    cond = jnp.zeros((n_rows, d), F32).at[:bsz].set(c).at[bsz].set(c_ctx)
    mods = _adaln_mods(cond, ada_w, ada_b)
    xa = jnp.concatenate([x.reshape(geo.n_lat, d), ctx.reshape(geo.n_ctx, d)], axis=0)

    mod = mods[0].reshape(n_rows, 1, 6 * d)
    proj = _norm_proj(xa, geo, norm_g[0, 0], mod, attn_w_in[0].astype(BF16), BF16, 1024)
    qk = _qk_prep(proj, geo, da_q_norm_g[0], da_k_norm_g[0])
    lam_init = 0.8 - 0.6 * math.exp(-0.3 * 0)
    lv = da_lambda[0].astype(F32)
    lam = jnp.exp(jnp.sum(lv[0] * lv[1])) - jnp.exp(jnp.sum(lv[2] * lv[3])) + lam_init
    attn = _diff_attention(qk, proj, geo, lam, da_subln_g[0], lam_init)
    gm = _gmlp(proj, geo, gm_ln_g[0], gm_ln_b[0], gm_w_s[0], gm_b_s[0])
    rw, rb = _router_operands(router_w[0], router_b[0])
    xn, hp, logits = _attn_out(attn, gm, attn_w_o[0].astype(BF16), xa, geo, mod, norm_g[0, 1], rw, rb)
    xa = _moe_layer(xn, hp, logits, geo, geo.m, mod, moe_w1, moe_b1, moe_w2, moe_b2, 0)

    mod = mods[1].reshape(n_rows, 1, 6 * d)
    w_in = ssd_w_in[0].astype(BF16)
    n_zx = SSD_D_INNER + SSD_CONV_CH
    zx = _norm_proj(xa, geo, norm_g[1, 0], mod, w_in[:, :n_zx], BF16, 1024)
    dt_raw = _norm_proj(xa, geo, norm_g[1, 0], mod, w_in[:, n_zx:], F32, SSD_DT_W)
    dt, ac = _ssd_prep(dt_raw, geo, ssd_dt_bias[0], ssd_a_log[0])
    cv = _conv_silu(zx, geo, ssd_conv_w[0], ssd_conv_b[0])
    yd = _ssd_scan(cv, dt, ac, geo)
    skip_x = jnp.repeat(ssd_d_skip[0].astype(F32).reshape(-1), SSD_HEAD_DIM).reshape(1, SSD_D_INNER)
    rw, rb = _router_operands(router_w[1], router_b[1])
    xn, hp, logits = _ssd_out(yd, cv, zx, skip_x, ssd_norm_g[0], ssd_w_o[0].astype(BF16), xa, geo, mod,
                              norm_g[1, 1], rw, rb)
    out = _moe_layer(xn, hp, logits, geo, geo.n_lat, mod, moe_w1, moe_b1, moe_w2, moe_b2, 1)
    return out.reshape(bsz, n, d)
```
